```python
import math
import jax
import jax.numpy as jnp
from jax import lax
import numpy as np


D_MODEL = 2048
BATCH = 8
SEQ = 2048
DEPTH = 2

PLE_DIM = 256
D_FF = 5632
NORM_EPS = 1e-6
MIX_WIDTH = D_MODEL // 2

RWKV_HEAD_DIM = 64
RWKV_HEADS = MIX_WIDTH // RWKV_HEAD_DIM
RWKV_WIDTH = RWKV_HEADS * RWKV_HEAD_DIM
RWKV_DECAY_RANK = 64
RWKV_A_RANK = 64
RWKV_GATE_RANK = 160
RWKV_DECAY_SCALE = math.exp(-0.5)
RWKV_GN_EPS = 64e-5
RWKV_SIZES = (RWKV_WIDTH, RWKV_WIDTH, RWKV_WIDTH, RWKV_DECAY_RANK, RWKV_DECAY_RANK, RWKV_A_RANK, RWKV_A_RANK, RWKV_GATE_RANK)
RWKV_COLS = sum(RWKV_SIZES)

DN_HEAD_DIM = 128
DN_HEADS = MIX_WIDTH // DN_HEAD_DIM
DN_WIDTH = DN_HEADS * DN_HEAD_DIM
DN_CONV = 5
DN_CHUNK = 64
DN_SIZES = (3 * DN_WIDTH, DN_WIDTH, DN_HEADS, DN_HEADS, DN_HEADS, DN_HEADS)
DN_COLS = sum(DN_SIZES)

S5_GROUP_CH = 16
S5_GROUPS = MIX_WIDTH // S5_GROUP_CH
S5_WIDTH = S5_GROUPS * S5_GROUP_CH
S5_STATE = 64

N_BRANCH = 3
GATE_COLS = N_BRANCH * D_MODEL
IN_SIZES = (RWKV_COLS, DN_COLS, S5_WIDTH, GATE_COLS)
IN_COLS = sum(IN_SIZES)

kernel_name = "hybrid_rwkv7_gdn_s5_encoder"


def split_cols(c, sizes):
    return jnp.split(c, [int(s) for s in np.cumsum(sizes)[:-1]], axis=-1)


def rmsnorm(x, g):
    xf = x.astype(jnp.float32)
    y = xf * lax.rsqrt(jnp.mean(xf * xf, axis=-1, keepdims=True) + NORM_EPS)
    return (y * g.astype(jnp.float32)).astype(x.dtype)


def l2norm(x):
    return x * lax.rsqrt(jnp.sum(x * x, axis=-1, keepdims=True) + 1e-6)


def swiglu(x, w1, w3, w2):
    return (jax.nn.silu(x @ w1) * (x @ w3)) @ w2


def centred_token_shift(c, mu):
    prev = jnp.pad(c[:, :-1], ((0, 0), (1, 0), (0, 0)))
    nxt = jnp.pad(c[:, 1:], ((0, 0), (0, 1), (0, 0)))
    return c + (0.5 * (prev + nxt) - c) * mu


def rwkv7_scan(r, w, k, v, kk, a, reverse):
    Bsz, L, H, N = r.shape
    xs = tuple(jnp.moveaxis(t, 1, 0) for t in (r, w, k, v, kk, a))

    def step(S, inp):
        r_t, w_t, k_t, v_t, kk_t, a_t = inp
        sa = jnp.einsum('bhvk,bhk->bhv', S, -kk_t)
        S = (S * w_t[:, :, None, :]
             + sa[..., None] * (kk_t * a_t)[:, :, None, :]
             + v_t[..., None] * k_t[:, :, None, :])
        return S, jnp.einsum('bhvk,bhk->bhv', S, r_t)

    S0 = jnp.zeros((Bsz, H, N, N), r.dtype)
    _, y = lax.scan(step, S0, xs, reverse=reverse)
    return jnp.moveaxis(y, 0, 1)


def rwkv7_mixer(c, mu, w0, w2, a0, a2, g2, k_k, k_a, r_k, ln_w, ln_b):
    f32 = jnp.float32
    Bsz, L, _ = c.shape
    c = centred_token_shift(c.astype(f32), mu.astype(f32))
    r, k, v, lw_f, lw_b, la_f, la_b, lg = split_cols(c, RWKV_SIZES)
    heads = lambda t: t.reshape(Bsz, L, RWKV_HEADS, RWKV_HEAD_DIM)
    r_h, v_h = heads(r), heads(v)
    kk = l2norm(heads(k * k_k))
    gate = jax.nn.sigmoid(lg) @ g2
    y = jnp.zeros_like(r_h)
    bonus = jnp.zeros_like(r_h)
    for d, (lw, la, rev) in enumerate(((lw_f, la_f, False), (lw_b, la_b, True))):
        w = jnp.exp(-RWKV_DECAY_SCALE * jax.nn.sigmoid(w0[d] + jnp.tanh(lw) @ w2[d]))
        a = jax.nn.sigmoid(a0[d] + la @ a2[d])
        k_d = heads(k * (1.0 + (a - 1.0) * k_a))
        y = y + rwkv7_scan(r_h, heads(w), k_d, v_h, kk, heads(a), rev)
        bonus = bonus + jnp.sum(r_h * k_d * r_k, axis=-1, keepdims=True) * v_h
    mean = jnp.mean(y, axis=-1, keepdims=True)
    var = jnp.mean(jnp.square(y - mean), axis=-1, keepdims=True)
    y = (y - mean) * lax.rsqrt(var + RWKV_GN_EPS) * ln_w + ln_b
    return (y + bonus).reshape(Bsz, L, RWKV_WIDTH) * gate


def depthwise_conv_centred(x, w):
    K, C = w.shape
    return lax.conv_general_dilated(
        x, w[:, None, :].astype(x.dtype), window_strides=(1,), padding=[(K // 2, K // 2)],
        dimension_numbers=('NWC', 'WIO', 'NWC'), feature_group_count=C)


def gated_delta_chunked(q, k, v, g, beta):
    Bsz, L, H, Dk = q.shape
    Dv = v.shape[-1]
    C = DN_CHUNK
    N = L // C

    def chunks(t):
        t = jnp.moveaxis(t, 2, 1)
        return t.reshape((Bsz, H, N, C) + t.shape[3:])

    q = chunks(q * Dk ** -0.5)
    k, v, g, beta = chunks(k), chunks(v), chunks(g), chunks(beta)
    gc = jnp.cumsum(g, axis=-1)
    tril = jnp.tril(jnp.ones((C, C), bool))
    strict = jnp.tril(jnp.ones((C, C), bool), -1)
    diff = gc[..., :, None] - gc[..., None, :]
    decay = jnp.where(tril, jnp.exp(jnp.where(tril, diff, 0.0)), 0.0)
    k_beta = k * beta[..., None]
    lower = jnp.where(strict, jnp.einsum('bhnid,bhnjd->bhnij', k_beta, k) * decay, 0.0)
    eye = jnp.eye(C, dtype=q.dtype)
    t_inv = lax.linalg.triangular_solve(eye + lower, jnp.broadcast_to(eye, lower.shape),
                                        left_side=True, lower=True, unit_diagonal=True)
    u = t_inv @ (v * beta[..., None])
    w = t_inv @ (k_beta * jnp.exp(gc)[..., None])
    qk = jnp.einsum('bhnid,bhnjd->bhnij', q, k) * decay
    g_last = gc[..., -1]

    def step(S, inp):
        q_n, k_n, u_n, w_n, qk_n, gc_n, gl_n = inp
        v_new = u_n - w_n @ S
        o = (q_n * jnp.exp(gc_n)[..., None]) @ S + qk_n @ v_new
        k_dec = k_n * jnp.exp(gl_n[..., None] - gc_n)[..., None]
        S = S * jnp.exp(gl_n)[..., None, None] + jnp.einsum('bhcd,bhce->bhde', k_dec, v_new)
        return S, o

    xs = tuple(jnp.moveaxis(t, 2, 0) for t in (q, k, u, w, qk, gc, g_last))
    S0 = jnp.zeros((Bsz, H, Dk, Dv), q.dtype)
    _, o = lax.scan(step, S0, xs)
    o = jnp.moveaxis(o, 0, 2).reshape(Bsz, H, L, Dv)
    return jnp.moveaxis(o, 1, 2)


def gdn_mixer(c, conv_w, a_log, dt_bias, norm_w):
    f32 = jnp.float32
    Bsz, L, _ = c.shape
    qkv, z, b_f, b_b, a_f, a_b = split_cols(c.astype(f32), DN_SIZES)
    qkv = jax.nn.silu(depthwise_conv_centred(qkv, conv_w.astype(f32)))
    q, k, v = jnp.split(qkv, 3, axis=-1)
    heads = lambda t: t.reshape(Bsz, L, DN_HEADS, DN_HEAD_DIM)
    q, k, v = l2norm(heads(q)), l2norm(heads(k)), heads(v)
    flip = lambda t: jnp.flip(t, axis=1)
    o = jnp.zeros_like(v)
    for d, (bt, at) in enumerate(((b_f, a_f), (b_b, a_b))):
        beta = jax.nn.sigmoid(bt)
        g = -jnp.exp(a_log[d].astype(f32)) * jax.nn.softplus(at + dt_bias[d])
        if d == 0:
            o = o + gated_delta_chunked(q, k, v, g, beta)
        else:
            o = o + flip(gated_delta_chunked(flip(q), flip(k), flip(v), flip(g), flip(beta)))
    o = o * lax.rsqrt(jnp.mean(o * o, axis=-1, keepdims=True) + NORM_EPS) * norm_w
    return (o * jax.nn.silu(heads(z))).reshape(Bsz, L, DN_WIDTH)


def s5_combine(e1, e2):
    a1, b1 = e1
    a2, b2 = e2
    return a1 * a2, a2 * b1 + b2


def s5_scan(u, a_re, a_im, log_dt, b_re, b_im, reverse):
    f32 = jnp.float32
    lam = lax.complex(a_re.astype(f32), a_im.astype(f32))
    dt = jnp.exp(log_dt.astype(f32))[:, None]
    a_bar = jnp.exp(lam * dt)
    b_bar = ((a_bar - 1.0) / lam)[..., None] * lax.complex(b_re.astype(f32), b_im.astype(f32))
    bu = jnp.einsum('blgc,gpc->blgp', u, b_bar)
    a_seq = jnp.broadcast_to(a_bar, bu.shape)
    _, xs = lax.associative_scan(s5_combine, (a_seq, bu), axis=1, reverse=reverse)
    return xs


def s5_mixer(u, a_re, a_im, log_dt, b_re, b_im, c_re, c_im, d_skip, glu_w, glu_b):
    f32 = jnp.float32
    Bsz, L, _ = u.shape
    u = u.astype(f32)
    ug = u.reshape(Bsz, L, S5_GROUPS, S5_GROUP_CH).astype(jnp.complex64)
    xs = (s5_scan(ug, a_re[0], a_im[0], log_dt[0], b_re[0], b_im[0], False)
          + s5_scan(ug, a_re[1], a_im[1], log_dt[1], b_re[1], b_im[1], True))
    c_mat = lax.complex(c_re.astype(f32), c_im.astype(f32))
    y = jnp.real(jnp.einsum('blgp,gcp->blgc', xs, c_mat)).reshape(Bsz, L, S5_WIDTH) + d_skip * u
    y = jax.nn.gelu(y)
    return y * jax.nn.sigmoid(y @ glu_w + glu_b)


def setup_inputs(seed: int = 0) -> dict:
    key = jax.random.key(seed)
    ks = iter(jax.random.split(key, 64))

    def nrm(shape, scale=1.0):
        return scale * jax.random.normal(next(ks), shape, jnp.float32)

    def unif(shape, lo, hi):
        return jax.random.uniform(next(ks), shape, jnp.float32, lo, hi)

    L = DEPTH
    d = {}
    d['x'] = nrm((BATCH, SEQ, D_MODEL))
    d['p'] = nrm((DEPTH, BATCH, SEQ, PLE_DIM))
    d['ffn1_norm'] = 1.0 + nrm((L, D_MODEL), 0.02)
    d['ffn1_w1'] = nrm((L, D_MODEL, D_FF), D_MODEL ** -0.5)
    d['ffn1_w3'] = nrm((L, D_MODEL, D_FF), D_MODEL ** -0.5)
    d['ffn1_w2'] = nrm((L, D_FF, D_MODEL), D_FF ** -0.5)
    d['mix_norm'] = 1.0 + nrm((L, D_MODEL), 0.02)
    d['w_in'] = nrm((L, D_MODEL, IN_COLS), D_MODEL ** -0.5)
    d['rwkv_mu'] = unif((L, RWKV_COLS), 0.0, 1.0)
    d['rwkv_w0'] = nrm((L, 2, RWKV_WIDTH), 1.0)
    d['rwkv_w2'] = nrm((L, 2, RWKV_DECAY_RANK, RWKV_WIDTH), 0.5 * RWKV_DECAY_RANK ** -0.5)
    d['rwkv_a0'] = nrm((L, 2, RWKV_WIDTH), 0.5)
    d['rwkv_a2'] = nrm((L, 2, RWKV_A_RANK, RWKV_WIDTH), RWKV_A_RANK ** -0.5)
    d['rwkv_g2'] = nrm((L, RWKV_GATE_RANK, RWKV_WIDTH), RWKV_GATE_RANK ** -0.5)
    d['rwkv_k_k'] = 1.0 + nrm((L, RWKV_WIDTH), 0.1)
    d['rwkv_k_a'] = 1.0 + nrm((L, RWKV_WIDTH), 0.1)
    d['rwkv_r_k'] = nrm((L, RWKV_HEADS, RWKV_HEAD_DIM), 0.1)
    d['rwkv_ln_w'] = 1.0 + nrm((L, RWKV_HEADS, RWKV_HEAD_DIM), 0.02)
    d['rwkv_ln_b'] = nrm((L, RWKV_HEADS, RWKV_HEAD_DIM), 0.02)
    d['dn_conv'] = nrm((L, DN_CONV, 3 * DN_WIDTH), DN_CONV ** -0.5)
    d['dn_a_log'] = jnp.log(unif((L, 2, DN_HEADS), 1.0, 16.0))
    dt = jnp.exp(unif((L, 2, DN_HEADS), math.log(1e-3), math.log(1e-1)))
    d['dn_dt_bias'] = dt + jnp.log(-jnp.expm1(-dt))
    d['dn_norm'] = 1.0 + nrm((L, DN_HEAD_DIM), 0.02)
    d['s5_a_re'] = -0.5 + nrm((L, 2, S5_GROUPS, S5_STATE), 0.01)
    d['s5_a_im'] = math.pi * jnp.arange(S5_STATE, dtype=jnp.float32) + nrm((L, 2, S5_GROUPS, S5_STATE), 0.01)
    d['s5_log_dt'] = unif((L, 2, S5_GROUPS), math.log(1e-3), math.log(1e-1))
    d['s5_b_re'] = nrm((L, 2, S5_GROUPS, S5_STATE, S5_GROUP_CH), (2 * S5_GROUP_CH) ** -0.5)
    d['s5_b_im'] = nrm((L, 2, S5_GROUPS, S5_STATE, S5_GROUP_CH), (2 * S5_GROUP_CH) ** -0.5)
    d['s5_c_re'] = nrm((L, S5_GROUPS, S5_GROUP_CH, S5_STATE), S5_STATE ** -0.5)
    d['s5_c_im'] = nrm((L, S5_GROUPS, S5_GROUP_CH, S5_STATE), S5_STATE ** -0.5)
    d['s5_d'] = nrm((L, S5_WIDTH), 1.0)
    d['s5_glu_w'] = nrm((L, S5_WIDTH, S5_WIDTH), S5_WIDTH ** -0.5)
    d['s5_glu_b'] = nrm((L, S5_WIDTH), 0.01)
    d['w_branch_a'] = nrm((L, RWKV_WIDTH, D_MODEL), RWKV_WIDTH ** -0.5)
    d['w_branch_b'] = nrm((L, DN_WIDTH, D_MODEL), DN_WIDTH ** -0.5)
    d['w_branch_c'] = nrm((L, S5_WIDTH, D_MODEL), S5_WIDTH ** -0.5)
    d['w_out'] = nrm((L, D_MODEL, D_MODEL), D_MODEL ** -0.5)
    d['ffn2_norm'] = 1.0 + nrm((L, D_MODEL), 0.02)
    d['ffn2_w1'] = nrm((L, D_MODEL, D_FF), D_MODEL ** -0.5)
    d['ffn2_w3'] = nrm((L, D_MODEL, D_FF), D_MODEL ** -0.5)
    d['ffn2_w2'] = nrm((L, D_FF, D_MODEL), D_FF ** -0.5)
    d['ple_norm'] = 1.0 + nrm((L, D_MODEL), 0.02)
    d['ple_w_gate'] = nrm((L, D_MODEL, D_MODEL), D_MODEL ** -0.5)
    d['ple_w_proj'] = nrm((L, PLE_DIM, D_MODEL), PLE_DIM ** -0.5)
    d['final_norm'] = 1.0 + nrm((D_MODEL,), 0.02)
    return d


def reference(x, p, ffn1_norm, ffn1_w1, ffn1_w3, ffn1_w2, mix_norm, w_in,
              rwkv_mu, rwkv_w0, rwkv_w2, rwkv_a0, rwkv_a2, rwkv_g2, rwkv_k_k, rwkv_k_a, rwkv_r_k,
              rwkv_ln_w, rwkv_ln_b, dn_conv, dn_a_log, dn_dt_bias, dn_norm,
              s5_a_re, s5_a_im, s5_log_dt, s5_b_re, s5_b_im, s5_c_re, s5_c_im, s5_d, s5_glu_w, s5_glu_b,
              w_branch_a, w_branch_b, w_branch_c, w_out,
              ffn2_norm, ffn2_w1, ffn2_w3, ffn2_w2, ple_norm, ple_w_gate, ple_w_proj, final_norm):
    Bsz, L, _ = x.shape
    for i in range(DEPTH):
        x = x + 0.5 * swiglu(rmsnorm(x, ffn1_norm[i]), ffn1_w1[i], ffn1_w3[i], ffn1_w2[i])
        h = rmsnorm(x, mix_norm[i])
        c_rwkv, c_dn, c_s5, c_gate = split_cols(h @ w_in[i], IN_SIZES)
        y_a = rwkv7_mixer(c_rwkv, rwkv_mu[i], rwkv_w0[i], rwkv_w2[i], rwkv_a0[i], rwkv_a2[i], rwkv_g2[i],
                          rwkv_k_k[i], rwkv_k_a[i], rwkv_r_k[i], rwkv_ln_w[i], rwkv_ln_b[i]).astype(x.dtype)
        y_b = gdn_mixer(c_dn, dn_conv[i], dn_a_log[i], dn_dt_bias[i], dn_norm[i]).astype(x.dtype)
        y_c = s5_mixer(c_s5, s5_a_re[i], s5_a_im[i], s5_log_dt[i], s5_b_re[i], s5_b_im[i],
                       s5_c_re[i], s5_c_im[i], s5_d[i], s5_glu_w[i], s5_glu_b[i]).astype(x.dtype)
        gates = jax.nn.sigmoid(c_gate.astype(jnp.float32)).astype(x.dtype).reshape(Bsz, L, N_BRANCH, D_MODEL)
        merged = (gates[:, :, 0] * (y_a @ w_branch_a[i])
                  + gates[:, :, 1] * (y_b @ w_branch_b[i])
                  + gates[:, :, 2] * (y_c @ w_branch_c[i]))
        x = x + merged @ w_out[i]
        x = x + 0.5 * swiglu(rmsnorm(x, ffn2_norm[i]), ffn2_w1[i], ffn2_w3[i], ffn2_w2[i])
        x = x + (p[i] @ ple_w_proj[i]) * jax.nn.sigmoid(rmsnorm(x, ple_norm[i]) @ ple_w_gate[i])
    return rmsnorm(x, final_norm)
```

```python
import functools
import math

import jax
import jax.numpy as jnp
from jax import lax
from jax.experimental import pallas as pl
from jax.experimental.pallas import tpu as pltpu

f32 = jnp.float32
bf16 = jnp.bfloat16
HI = lax.Precision.HIGHEST

D_MODEL = 2048
D_FF = 5632
PLE_DIM = 256
NORM_EPS = 1e-6
MIX = 1024

RW_HEADS = 16
RW_HD = 64
RW_DECAY_SCALE = math.exp(-0.5)
RW_GN_EPS = 64e-5
RW_COLS = 3584
RW_CHUNK = 64

DN_HEADS = 8
DN_HD = 128
DN_CONV = 5
DN_CHUNK = 64

S5_G = 64
S5_CG = 16
S5_P = 64
S5_CHUNK = 16

VMEM_LIMIT = 56 * 1024 * 1024


def _cparams(n_axes):
    return pltpu.CompilerParams(
        dimension_semantics=("arbitrary",) * n_axes, vmem_limit_bytes=VMEM_LIMIT
    )


def _mm(a, b):
    return jnp.dot(a.astype(bf16), b.astype(bf16), preferred_element_type=f32)


def _mm_nt(a, b):
    return lax.dot_general(
        a.astype(bf16), b.astype(bf16), (((1,), (1,)), ((), ())), preferred_element_type=f32
    )


def _mm_tn(a, b):
    return lax.dot_general(
        a.astype(bf16), b.astype(bf16), (((0,), (0,)), ((), ())), preferred_element_type=f32
    )


def _mm_hi(a, b):
    return jnp.dot(a, b, precision=HI, preferred_element_type=f32)


def _mm2(a, b_bf16):
    hi = a.astype(bf16)
    lo = (a - hi.astype(f32)).astype(bf16)
    return jnp.dot(hi, b_bf16, preferred_element_type=f32) + jnp.dot(
        lo, b_bf16, preferred_element_type=f32
    )


def _rms(x, g):
    return x * lax.rsqrt(jnp.mean(x * x, axis=-1, keepdims=True) + NORM_EPS) * g


def _sigmoid(x):
    return 1.0 / (1.0 + jnp.exp(-x))


def _silu(x):
    return x * _sigmoid(x)


def _softplus(x):
    return jnp.maximum(x, 0.0) + jnp.log(1.0 + jnp.exp(-jnp.abs(x)))


def _ffn_kernel(x_ref, g_ref, w1_ref, w3_ref, w2_ref, o_ref, h_ref):
    @pl.when(pl.program_id(1) == 0)
    def _():
        x = x_ref[...]
        h_ref[...] = _rms(x, g_ref[...]).astype(bf16)
        o_ref[...] = x

    h = h_ref[...]
    a = jnp.dot(h, w1_ref[...], preferred_element_type=f32)
    b = jnp.dot(h, w3_ref[...], preferred_element_type=f32)
    act = (0.5 * _silu(a) * b).astype(bf16)
    o_ref[...] += jnp.dot(act, w2_ref[...], preferred_element_type=f32)


def _ffn(x, g, w1, w3, w2, tm=512, tf=512):
    t = x.shape[0]
    tm = min(tm, t)
    return pl.pallas_call(
        _ffn_kernel,
        out_shape=jax.ShapeDtypeStruct((t, D_MODEL), f32),
        grid=(t // tm, D_FF // tf),
        in_specs=[
            pl.BlockSpec((tm, D_MODEL), lambda i, j: (i, 0)),
            pl.BlockSpec((1, D_MODEL), lambda i, j: (0, 0)),
            pl.BlockSpec((D_MODEL, tf), lambda i, j: (0, j)),
            pl.BlockSpec((D_MODEL, tf), lambda i, j: (0, j)),
            pl.BlockSpec((tf, D_MODEL), lambda i, j: (j, 0)),
        ],
        out_specs=pl.BlockSpec((tm, D_MODEL), lambda i, j: (i, 0)),
        scratch_shapes=[pltpu.VMEM((tm, D_MODEL), bf16)],
        compiler_params=_cparams(2),
        name="ffn",
    )(x, g, w1, w3, w2)


def _norm_mm_kernel(x_ref, g_ref, w_ref, o_ref, h_ref, *, act):
    @pl.when(pl.program_id(1) == 0)
    def _():
        h_ref[...] = _rms(x_ref[...], g_ref[...]).astype(bf16)

    y = jnp.dot(h_ref[...], w_ref[...], preferred_element_type=f32)
    if act == "sigmoid":
        y = _sigmoid(y)
    o_ref[...] = y.astype(o_ref.dtype)


def _norm_mm(x, g, w, tn, act=None, out_dtype=f32, tm=512):
    t = x.shape[0]
    n = w.shape[1]
    tm = min(tm, t)
    return pl.pallas_call(
        functools.partial(_norm_mm_kernel, act=act),
        out_shape=jax.ShapeDtypeStruct((t, n), out_dtype),
        grid=(t // tm, n // tn),
        in_specs=[
            pl.BlockSpec((tm, D_MODEL), lambda i, j: (i, 0)),
            pl.BlockSpec((1, D_MODEL), lambda i, j: (0, 0)),
            pl.BlockSpec((D_MODEL, tn), lambda i, j: (0, j)),
        ],
        out_specs=pl.BlockSpec((tm, tn), lambda i, j: (i, j)),
        scratch_shapes=[pltpu.VMEM((tm, D_MODEL), bf16)],
        compiler_params=_cparams(2),
        name="norm_mm",
    )(x, g, w)


def _merge_kernel(x_ref, ya_ref, yb_ref, yc_ref, sg_ref, wa_ref, wb_ref, wc_ref, wo_ref, o_ref):
    @pl.when(pl.program_id(1) == 0)
    def _():
        o_ref[...] = x_ref[...]

    sg = sg_ref[...]
    m = (
        sg[:, 0, :] * jnp.dot(ya_ref[...], wa_ref[...], preferred_element_type=f32)
        + sg[:, 1, :] * jnp.dot(yb_ref[...], wb_ref[...], preferred_element_type=f32)
        + sg[:, 2, :] * jnp.dot(yc_ref[...], wc_ref[...], preferred_element_type=f32)
    )
    o_ref[...] += jnp.dot(m.astype(bf16), wo_ref[...], preferred_element_type=f32)


def _merge(x, ya, yb, yc, sg, wa, wb, wc, wo, tm=512, tn=512):
    t = x.shape[0]
    tm = min(tm, t)
    sg3 = sg.reshape(t, 3, D_MODEL)
    return pl.pallas_call(
        _merge_kernel,
        out_shape=jax.ShapeDtypeStruct((t, D_MODEL), f32),
        grid=(t // tm, D_MODEL // tn),
        in_specs=[
            pl.BlockSpec((tm, D_MODEL), lambda i, j: (i, 0)),
            pl.BlockSpec((tm, MIX), lambda i, j: (i, 0)),
            pl.BlockSpec((tm, MIX), lambda i, j: (i, 0)),
            pl.BlockSpec((tm, MIX), lambda i, j: (i, 0)),
            pl.BlockSpec((tm, 3, tn), lambda i, j: (i, 0, j)),
            pl.BlockSpec((MIX, tn), lambda i, j: (0, j)),
            pl.BlockSpec((MIX, tn), lambda i, j: (0, j)),
            pl.BlockSpec((MIX, tn), lambda i, j: (0, j)),
            pl.BlockSpec((tn, D_MODEL), lambda i, j: (j, 0)),
        ],
        out_specs=pl.BlockSpec((tm, D_MODEL), lambda i, j: (i, 0)),
        compiler_params=_cparams(2),
        name="merge",
    )(x, ya, yb, yc, sg3, wa, wb, wc, wo)


def _ple_kernel(x_ref, xj_ref, p_ref, g_ref, wg_ref, wp_ref, o_ref, h_ref):
    @pl.when(pl.program_id(1) == 0)
    def _():
        h_ref[...] = _rms(x_ref[...], g_ref[...]).astype(bf16)

    gate = _sigmoid(jnp.dot(h_ref[...], wg_ref[...], preferred_element_type=f32))
    emb = jnp.dot(p_ref[...].astype(bf16), wp_ref[...], preferred_element_type=f32)
    o_ref[...] = xj_ref[...] + emb * gate


def _ple(x, p, g, wg, wp, tm=512, tn=512):
    t = x.shape[0]
    tm = min(tm, t)
    return pl.pallas_call(
        _ple_kernel,
        out_shape=jax.ShapeDtypeStruct((t, D_MODEL), f32),
        grid=(t // tm, D_MODEL // tn),
        in_specs=[
            pl.BlockSpec((tm, D_MODEL), lambda i, j: (i, 0)),
            pl.BlockSpec((tm, tn), lambda i, j: (i, j)),
            pl.BlockSpec((tm, PLE_DIM), lambda i, j: (i, 0)),
            pl.BlockSpec((1, D_MODEL), lambda i, j: (0, 0)),
            pl.BlockSpec((D_MODEL, tn), lambda i, j: (0, j)),
            pl.BlockSpec((PLE_DIM, tn), lambda i, j: (0, j)),
        ],
        out_specs=pl.BlockSpec((tm, tn), lambda i, j: (i, j)),
        scratch_shapes=[pltpu.VMEM((tm, D_MODEL), bf16)],
        compiler_params=_cparams(2),
        name="ple",
    )(x, x, p, g, wg, wp)


def _final_norm_kernel(x_ref, g_ref, o_ref):
    o_ref[...] = _rms(x_ref[...], g_ref[...])


def _final_norm(x, g, tm=512):
    t = x.shape[0]
    tm = min(tm, t)
    return pl.pallas_call(
        _final_norm_kernel,
        out_shape=jax.ShapeDtypeStruct((t, D_MODEL), f32),
        grid=(t // tm,),
        in_specs=[
            pl.BlockSpec((tm, D_MODEL), lambda i: (i, 0)),
            pl.BlockSpec((1, D_MODEL), lambda i: (0, 0)),
        ],
        out_specs=pl.BlockSpec((tm, D_MODEL), lambda i: (i, 0)),
        compiler_params=_cparams(1),
        name="final_norm",
    )(x, g)


def _halo(prev_ref, c_ref, next_ref, n_tiles):
    i = pl.program_id(1)
    prev = jnp.where(i == 0, 0.0, prev_ref[...])
    nxt = jnp.where(i == n_tiles - 1, 0.0, next_ref[...])
    return jnp.concatenate([prev, c_ref[...], nxt], axis=0)


def _shifted(ext, k, tl):
    n = ext.shape[0]
    return pltpu.roll(ext, (-k) % n, axis=0)[8 : 8 + tl]


def _tri_masks(c, reverse):
    ii = lax.broadcasted_iota(jnp.int32, (c, c), 0)
    jj = lax.broadcasted_iota(jnp.int32, (c, c), 1)
    if reverse:
        ii, jj = jj, ii
    return ii, jj


def _unit_tri_inverse(m, c, reverse):
    ii, jj = _tri_masks(c, reverse)
    eye = (ii == jj).astype(f32)
    d = eye + jnp.where(((ii >> 1) == (jj >> 1)) & (ii > jj), m, 0.0)
    s, sh = 2, 1
    while s < c:
        blk = ((ii >> (sh + 1)) == (jj >> (sh + 1))) & (((ii >> sh) & 1) == 1) & (((jj >> sh) & 1) == 0)
        ms = jnp.where(blk, m, 0.0)
        d = d + _mm(d, _mm(ms, d))
        s, sh = s * 2, sh + 1
    return d


def _rwkv_prep_kernel(
    prev_ref, c_ref, next_ref, mu_ref, w0_ref, a0_ref, w2_ref, a2_ref, g2_ref, kk_ref, ka_ref, rk_ref, ones_ref,
    r_o, v_o, an_o, lwf_o, lwb_o, kf_o, kb_o, bf_o, bb_o, gate_o, bonus_o, *, tl, n_tiles
):
    ext = _halo(prev_ref, c_ref, next_ref, n_tiles)
    c = ext[8 : 8 + tl]
    cs = c + (0.5 * (_shifted(ext, -1, tl) + _shifted(ext, 1, tl)) - c) * mu_ref[...]
    r = cs[:, 0:MIX]
    k = cs[:, MIX : 2 * MIX]
    v = cs[:, 2 * MIX : 3 * MIX]
    lw_in = jnp.tanh(cs[:, 3 * MIX : 3 * MIX + 128])
    la_in = cs[:, 3 * MIX + 128 : 3 * MIX + 256]
    lg_in = _sigmoid(cs[:, 3 * MIX + 256 : 3 * MIX + 512])
    logw = -RW_DECAY_SCALE * _sigmoid(w0_ref[...] + _mm(lw_in, w2_ref[...]))
    a = _sigmoid(a0_ref[...] + _mm(la_in, a2_ref[...]))
    gate_o[...] = _mm(lg_in, g2_ref[...])
    ones = ones_ref[...]
    kx = k * kk_ref[...]
    kk = kx * lax.rsqrt(_mm2(kx * kx, ones) + 1e-6)
    a_f = a[:, 0:MIX]
    a_b = a[:, MIX : 2 * MIX]
    ka = ka_ref[...]
    k_f = k * (1.0 + (a_f - 1.0) * ka)
    k_b = k * (1.0 + (a_b - 1.0) * ka)
    r_o[...] = r
    v_o[...] = v
    an_o[...] = -kk
    lwf_o[...] = logw[:, 0:MIX]
    lwb_o[...] = logw[:, MIX : 2 * MIX]
    kf_o[...] = k_f
    kb_o[...] = k_b
    bf_o[...] = kk * a_f
    bb_o[...] = kk * a_b
    bonus_o[...] = _mm2(r * (k_f + k_b) * rk_ref[...], ones) * v


def _rwkv_prep(c, mu, w0, a0, w2, a2, g2, k_k, k_a, r_k, ones64, tl=128):
    bsz, seq, _ = c.shape
    tl = min(tl, seq)
    n_tiles = seq // tl
    nb = tl // 8
    row = lambda a: pl.BlockSpec(a.shape, lambda b, i: (0,) * a.ndim)
    out = jax.ShapeDtypeStruct((bsz, seq, MIX), f32)
    ospec = pl.BlockSpec((None, tl, MIX), lambda b, i: (b, i, 0))
    return pl.pallas_call(
        functools.partial(_rwkv_prep_kernel, tl=tl, n_tiles=n_tiles),
        out_shape=[out] * 11,
        grid=(bsz, n_tiles),
        in_specs=[
            pl.BlockSpec((None, 8, RW_COLS), lambda b, i: (b, jnp.maximum(i * nb - 1, 0), 0)),
            pl.BlockSpec((None, tl, RW_COLS), lambda b, i: (b, i, 0)),
            pl.BlockSpec((None, 8, RW_COLS), lambda b, i: (b, jnp.minimum((i + 1) * nb, seq // 8 - 1), 0)),
            row(mu), row(w0), row(a0), row(w2), row(a2), row(g2), row(k_k), row(k_a), row(r_k), row(ones64),
        ],
        out_specs=[ospec] * 11,
        compiler_params=_cparams(2),
        name="rwkv_prep",
    )(c, c, c, mu, w0, a0, w2, a2, g2, k_k, k_a, r_k, ones64)


def _rwkv_chunk(s, r, kd, v, an, b, lw, reverse):
    c = RW_CHUNK
    ii, jj = _tri_masks(c, reverse)
    incl = jj <= ii
    strict = jj < ii
    cl = _mm_hi(incl.astype(f32), lw)
    e_in = jnp.exp(cl)
    e_neg = jnp.exp(-cl)
    at = an * jnp.exp(cl - lw)
    bt = b * e_neg
    kt = kd * e_neg
    rt = r * e_in
    a_ab = jnp.where(strict, _mm_nt(at, bt), 0.0)
    a_ak = jnp.where(strict, _mm_nt(at, kt), 0.0)
    p_b = jnp.where(incl, _mm_nt(rt, bt), 0.0)
    p_k = jnp.where(incl, _mm_nt(rt, kt), 0.0)
    tinv = _unit_tri_inverse(a_ab, c, reverse)
    w = _mm(tinv, at)
    u0 = _mm(tinv, _mm(a_ak, v))
    u = _mm_nt(w, s) + u0
    y = _mm_nt(rt, s) + _mm(p_b, u) + _mm(p_k, v)
    cl_last = cl[0:1, :] if reverse else cl[c - 1 : c, :]
    e_last = jnp.exp(cl_last - cl)
    s_new = s * jnp.exp(cl_last) + _mm_tn(u, b * e_last) + _mm_tn(v, kd * e_last)
    return y, s_new


def _rwkv_scan_kernel(r_ref, v_ref, an_ref, lwf_ref, lwb_ref, kf_ref, kb_ref, bf_ref, bb_ref, y_ref, s_ref, *, n_chunks):
    c = RW_CHUNK
    s_ref[...] = jnp.zeros_like(s_ref)
    y_ref[...] = jnp.zeros_like(y_ref)

    def body(n, carry):
        for d, (lw_ref, k_ref, b_ref) in enumerate(((lwf_ref, kf_ref, bf_ref), (lwb_ref, kb_ref, bb_ref))):
            ci = n if d == 0 else n_chunks - 1 - n
            rows = pl.ds(pl.multiple_of(ci * c, c), c)
            for h in range(2):
                sl = slice(h * RW_HD, (h + 1) * RW_HD)
                y, s_new = _rwkv_chunk(s_ref[2 * d + h], r_ref[rows, sl], k_ref[rows, sl], v_ref[rows, sl],
                                       an_ref[rows, sl], b_ref[rows, sl], lw_ref[rows, sl], d == 1)
                s_ref[2 * d + h] = s_new
                y_ref[rows, sl] += y
        return carry

    lax.fori_loop(0, n_chunks, body, 0)


def _rwkv_scan(r, v, an, lwf, lwb, kf, kb, bf_, bb):
    bsz, seq, _ = r.shape
    spec = pl.BlockSpec((None, seq, 128), lambda b, h: (b, 0, h))
    return pl.pallas_call(
        functools.partial(_rwkv_scan_kernel, n_chunks=seq // RW_CHUNK),
        out_shape=jax.ShapeDtypeStruct((bsz, seq, MIX), f32),
        grid=(bsz, MIX // 128),
        in_specs=[spec] * 9,
        out_specs=spec,
        scratch_shapes=[pltpu.VMEM((4, RW_HD, RW_HD), f32)],
        compiler_params=_cparams(2),
        name="rwkv_scan",
    )(r, v, an, lwf, lwb, kf, kb, bf_, bb)


def _rwkv_post_kernel(y_ref, bonus_ref, gate_ref, lnw_ref, lnb_ref, avg_ref, o_ref):
    y = y_ref[...]
    avg = avg_ref[...]
    mean = _mm2(y, avg)
    yc = y - mean
    var = _mm2(yc * yc, avg)
    yn = yc * lax.rsqrt(var + RW_GN_EPS) * lnw_ref[...] + lnb_ref[...]
    o_ref[...] = ((yn + bonus_ref[...]) * gate_ref[...]).astype(o_ref.dtype)


def _rwkv_post(y, bonus, gate, ln_w, ln_b, avg64, tm=512):
    t = y.shape[0]
    tm = min(tm, t)
    tok = pl.BlockSpec((tm, MIX), lambda i: (i, 0))
    row = pl.BlockSpec((1, MIX), lambda i: (0, 0))
    return pl.pallas_call(
        _rwkv_post_kernel,
        out_shape=jax.ShapeDtypeStruct((t, MIX), bf16),
        grid=(t // tm,),
        in_specs=[tok, tok, tok, row, row, pl.BlockSpec((MIX, MIX), lambda i: (0, 0))],
        out_specs=tok,
        compiler_params=_cparams(1),
        name="rwkv_post",
    )(y, bonus, gate, ln_w, ln_b, avg64)


def _dn_prep_kernel(prev_ref, c_ref, next_ref, cw_ref, ones_ref, q_o, k_o, v_o, *, tl, n_tiles):
    ext = _halo(prev_ref, c_ref, next_ref, n_tiles)
    cw = cw_ref[...]
    acc = None
    for j in range(DN_CONV):
        term = _shifted(ext, j - DN_CONV // 2, tl) * cw[j : j + 1, :]
        acc = term if acc is None else acc + term
    qkv = _silu(acc)
    ones = ones_ref[...]
    q = qkv[:, 0:MIX]
    k = qkv[:, MIX : 2 * MIX]
    q_o[...] = q * lax.rsqrt(_mm2(q * q, ones) + 1e-6) * (DN_HD ** -0.5)
    k_o[...] = k * lax.rsqrt(_mm2(k * k, ones) + 1e-6)
    v_o[...] = qkv[:, 2 * MIX : 3 * MIX]


def _dn_prep(c, conv_w8, ones128, tl=128):
    bsz, seq, _ = c.shape
    tl = min(tl, seq)
    n_tiles = seq // tl
    nb = tl // 8
    w = 3 * MIX
    out = jax.ShapeDtypeStruct((bsz, seq, MIX), f32)
    ospec = pl.BlockSpec((None, tl, MIX), lambda b, i: (b, i, 0))
    return pl.pallas_call(
        functools.partial(_dn_prep_kernel, tl=tl, n_tiles=n_tiles),
        out_shape=[out] * 3,
        grid=(bsz, n_tiles),
        in_specs=[
            pl.BlockSpec((None, 8, w), lambda b, i: (b, jnp.maximum(i * nb - 1, 0), 0)),
            pl.BlockSpec((None, tl, w), lambda b, i: (b, i, 0)),
            pl.BlockSpec((None, 8, w), lambda b, i: (b, jnp.minimum((i + 1) * nb, seq // 8 - 1), 0)),
            pl.BlockSpec((8, w), lambda b, i: (0, 0)),
            pl.BlockSpec((MIX, MIX), lambda b, i: (0, 0)),
        ],
        out_specs=[ospec] * 3,
        compiler_params=_cparams(2),
        name="dn_prep",
    )(c, c, c, conv_w8, ones128)


def _dn_chunk(s, q, k, v, beta_row, g_row, reverse):
    c = DN_CHUNK
    ii, jj = _tri_masks(c, reverse)
    incl = jj <= ii
    strict = jj < ii
    eye = ii == jj
    ones = jnp.ones((c, DN_HD), f32)
    gc = _mm_hi(jnp.where(incl, g_row, 0.0), ones)
    beta = _mm_hi(jnp.where(eye, beta_row, 0.0), ones)
    gc_row = _mm_hi(jnp.broadcast_to(g_row, (8, c)), _cum_matrix(c, reverse))[0:1, :]
    diff = gc[:, 0:c] - gc_row
    decay = jnp.where(incl, jnp.exp(jnp.where(incl, diff, 0.0)), 0.0)
    kb = k * beta
    nmat = jnp.where(strict, _mm_nt(kb, k) * decay, 0.0)
    tinv = _unit_tri_inverse(-nmat, c, reverse)
    e_gc = jnp.exp(gc)
    u = _mm(tinv, v * beta)
    w = _mm(tinv, kb * e_gc)
    qk = jnp.where(incl, _mm_nt(q, k) * decay, 0.0)
    v_new = u - _mm(w, s)
    o = _mm(q * e_gc, s) + _mm(qk, v_new)
    gl = gc[0:1, :] if reverse else gc[c - 1 : c, :]
    s_new = s * jnp.exp(gl) + _mm_tn(k * jnp.exp(gl - gc), v_new)
    return o, s_new


def _cum_matrix(c, reverse):
    ii = lax.broadcasted_iota(jnp.int32, (c, c), 0)
    jj = lax.broadcasted_iota(jnp.int32, (c, c), 1)
    return ((ii >= jj) if reverse else (ii <= jj)).astype(f32)


def _dn_scan_kernel(alog_ref, dtb_ref, q_ref, k_ref, v_ref, z_ref, gs_ref, nw_ref, o_ref, s_ref, acc_ref, *, n_chunks):
    c = DN_CHUNK
    h = pl.program_id(1)
    s_ref[...] = jnp.zeros_like(s_ref)
    acc_ref[...] = jnp.zeros_like(acc_ref)

    def body(n, carry):
        for d in range(2):
            ci = n if d == 0 else n_chunks - 1 - n
            rows = pl.ds(pl.multiple_of(ci * c, c), c)
            g8 = gs_ref[ci]
            beta_row = _sigmoid(g8[d : d + 1, :])
            a_neg = -jnp.exp(jnp.full((1, c), alog_ref[d, h], f32))
            g_row = a_neg * _softplus(g8[2 + d : 3 + d, :] + dtb_ref[d, h])
            o, s_new = _dn_chunk(s_ref[d], q_ref[rows, :], k_ref[rows, :], v_ref[rows, :], beta_row, g_row, d == 1)
            s_ref[d] = s_new
            acc_ref[rows, :] += o
        return carry

    lax.fori_loop(0, n_chunks, body, 0)
    o = acc_ref[...]
    o = o * lax.rsqrt(jnp.mean(o * o, axis=-1, keepdims=True) + NORM_EPS) * nw_ref[...]
    o_ref[...] = (o * _silu(z_ref[...])).astype(o_ref.dtype)


def _dn_scan(a_log, dt_bias, q, k, v, c_dn, gs, norm_w):
    bsz, seq, _ = q.shape
    n_chunks = seq // DN_CHUNK
    spec = pl.BlockSpec((None, seq, DN_HD), lambda b, h: (b, 0, h))
    smem = pl.BlockSpec(memory_space=pltpu.SMEM)
    return pl.pallas_call(
        functools.partial(_dn_scan_kernel, n_chunks=n_chunks),
        out_shape=jax.ShapeDtypeStruct((bsz, seq, MIX), bf16),
        grid=(bsz, DN_HEADS),
        in_specs=[
            smem, smem, spec, spec, spec,
            pl.BlockSpec((None, seq, DN_HD), lambda b, h: (b, 0, 3 * DN_HEADS + h)),
            pl.BlockSpec((None, None, n_chunks, 8, DN_CHUNK), lambda b, h: (b, h, 0, 0, 0)),
            pl.BlockSpec((1, DN_HD), lambda b, h: (0, 0)),
        ],
        out_specs=spec,
        scratch_shapes=[pltpu.VMEM((2, DN_HD, DN_HD), f32), pltpu.VMEM((seq, DN_HD), f32)],
        compiler_params=_cparams(2),
        name="dn_scan",
    )(a_log, dt_bias, q, k, v, c_dn, gs, norm_w)


def _s5_kernel(u_ref, tf_ref, tb_ref, frf_ref, fif_ref, frb_ref, fib_ref, erf_ref, eif_ref, erb_ref, eib_ref,
               pf_ref, pb_ref, y_ref, vr_ref, vi_ref, xr_ref, xi_ref, *, n_chunks, bsz):
    u = u_ref[...].astype(bf16)
    y = jnp.dot(u, tf_ref[...], preferred_element_type=f32) + jnp.dot(u, tb_ref[...], preferred_element_type=f32)
    for d, (fr, fi, er, ei, pw) in enumerate(
        ((frf_ref, fif_ref, erf_ref, eif_ref, pf_ref), (frb_ref, fib_ref, erb_ref, eib_ref, pb_ref))
    ):
        vr_ref[...] = jnp.dot(u, fr[...], preferred_element_type=f32)
        vi_ref[...] = jnp.dot(u, fi[...], preferred_element_type=f32)
        ar = pw[0:1, :]
        ai = pw[1:2, :]

        def body(n, carry, d=d, ar=ar, ai=ai):
            xr, xi = carry
            ci = n if d == 0 else n_chunks - 1 - n
            rows = pl.ds(pl.multiple_of(ci * bsz, bsz), bsz)
            xr_ref[rows, :] = xr
            xi_ref[rows, :] = xi
            return (ar * xr - ai * xi + vr_ref[rows, :], ar * xi + ai * xr + vi_ref[rows, :])

        zero = jnp.zeros((bsz, S5_P), f32)
        lax.fori_loop(0, n_chunks, body, (zero, zero))
        y = y + _mm(xr_ref[...], er[...]) + _mm(xi_ref[...], ei[...])
    y_ref[...] = y


def _s5_core(u_t, mats, bsz):
    g, m, w = u_t.shape
    n_chunks = m // bsz
    blk = lambda a: pl.BlockSpec((None,) + a.shape[1:], lambda i: (i,) + (0,) * (a.ndim - 1))
    return pl.pallas_call(
        functools.partial(_s5_kernel, n_chunks=n_chunks, bsz=bsz),
        out_shape=jax.ShapeDtypeStruct((g, m, w), f32),
        grid=(g,),
        in_specs=[blk(u_t)] + [blk(a) for a in mats],
        out_specs=blk(u_t),
        scratch_shapes=[pltpu.VMEM((m, S5_P), f32)] * 4,
        compiler_params=_cparams(1),
        name="s5_core",
    )(u_t, *mats)


def _s5_post_kernel(y_ref, u_ref, d_ref, w_ref, b_ref, o_ref):
    y = y_ref[...] + d_ref[...] * u_ref[...]
    y = 0.5 * y * (1.0 + jnp.tanh(math.sqrt(2.0 / math.pi) * (y + 0.044715 * (y * y * y))))
    o_ref[...] = (y * _sigmoid(_mm(y, w_ref[...]) + b_ref[...])).astype(o_ref.dtype)


def _s5_post(y, u, d_skip, glu_w, glu_b, tm=512):
    t = y.shape[0]
    tm = min(tm, t)
    tok = pl.BlockSpec((tm, MIX), lambda i: (i, 0))
    row = pl.BlockSpec((1, MIX), lambda i: (0, 0))
    return pl.pallas_call(
        _s5_post_kernel,
        out_shape=jax.ShapeDtypeStruct((t, MIX), bf16),
        grid=(t // tm,),
        in_specs=[tok, tok, row, pl.BlockSpec((MIX, MIX), lambda i: (0, 0)), row],
        out_specs=tok,
        compiler_params=_cparams(1),
        name="s5_post",
    )(y, u, d_skip, glu_w, glu_b)


def _s5_matrices(a_re, a_im, log_dt, b_re, b_im, c_re, c_im, reverse):
    c = S5_CHUNK
    dt = jnp.exp(log_dt)[:, None]
    mag = a_re * dt
    ph = a_im * dt

    def power(t):
        m = jnp.exp(mag[None] * t[:, None, None])
        return m * jnp.cos(ph[None] * t[:, None, None]), m * jnp.sin(ph[None] * t[:, None, None])

    a1r, a1i = power(jnp.ones((1,), f32))
    a1r, a1i = a1r[0], a1i[0]
    den = a_re * a_re + a_im * a_im
    qr = ((a1r - 1.0) * a_re + a1i * a_im) / den
    qi = (a1i * a_re - (a1r - 1.0) * a_im) / den
    bbr = qr[..., None] * b_re - qi[..., None] * b_im
    bbi = qr[..., None] * b_im + qi[..., None] * b_re
    steps = jnp.arange(c + 1, dtype=f32)
    pr, pi = power(steps)
    car = jnp.einsum("gcp,tgp->tgcp", c_re, pr, precision=HI) - jnp.einsum("gcp,tgp->tgcp", c_im, pi, precision=HI)
    cai = jnp.einsum("gcp,tgp->tgcp", c_re, pi, precision=HI) + jnp.einsum("gcp,tgp->tgcp", c_im, pr, precision=HI)
    kmat = jnp.einsum("tgcp,gpd->tgcd", car, bbr, precision=HI) - jnp.einsum("tgcp,gpd->tgcd", cai, bbi, precision=HI)
    jidx = jnp.arange(c)[:, None]
    iidx = jnp.arange(c)[None, :]
    lag = (jidx - iidx) if reverse else (iidx - jidx)
    valid = lag >= 0
    kt = kmat[jnp.clip(lag, 0, c)]
    kt = jnp.where(valid[:, :, None, None, None], kt, 0.0)
    tmat = kt.transpose(2, 0, 4, 1, 3).reshape(S5_G, c * S5_CG, c * S5_CG)
    fpow = jnp.arange(c) if reverse else (c - 1 - jnp.arange(c))
    fr = pr[fpow][:, :, :, None] * bbr[None] - pi[fpow][:, :, :, None] * bbi[None]
    fi = pr[fpow][:, :, :, None] * bbi[None] + pi[fpow][:, :, :, None] * bbr[None]
    fr = fr.transpose(1, 0, 3, 2).reshape(S5_G, c * S5_CG, S5_P)
    fi = fi.transpose(1, 0, 3, 2).reshape(S5_G, c * S5_CG, S5_P)
    epow = (c - jnp.arange(c)) if reverse else (jnp.arange(c) + 1)
    er = car[epow].transpose(1, 3, 0, 2).reshape(S5_G, S5_P, c * S5_CG)
    ei = (-cai[epow]).transpose(1, 3, 0, 2).reshape(S5_G, S5_P, c * S5_CG)
    pw = jnp.stack([pr[c], pi[c]], axis=1)
    pw = jnp.concatenate([pw, jnp.zeros((S5_G, 6, S5_P), f32)], axis=1)
    return tmat.astype(bf16), fr.astype(bf16), fi.astype(bf16), er.astype(bf16), ei.astype(bf16), pw


def _block_diag_ones(block, scale=1.0):
    idx = jnp.arange(MIX) // block
    return ((idx[:, None] == idx[None, :]).astype(f32) * scale).astype(bf16)


def kernel(x, p, ffn1_norm, ffn1_w1, ffn1_w3, ffn1_w2, mix_norm, w_in, rwkv_mu, rwkv_w0, rwkv_w2, rwkv_a0, rwkv_a2, rwkv_g2, rwkv_k_k, rwkv_k_a, rwkv_r_k, rwkv_ln_w, rwkv_ln_b, dn_conv, dn_a_log, dn_dt_bias, dn_norm, s5_a_re, s5_a_im, s5_log_dt, s5_b_re, s5_b_im, s5_c_re, s5_c_im, s5_d, s5_glu_w, s5_glu_b, w_branch_a, w_branch_b, w_branch_c, w_out, ffn2_norm, ffn2_w1, ffn2_w3, ffn2_w2, ple_norm, ple_w_gate, ple_w_proj, final_norm):
    bsz, seq, _ = x.shape
    t = bsz * seq
    depth = p.shape[0]
    row = lambda a: a.reshape(1, -1).astype(f32)
    ones64 = _block_diag_ones(RW_HD)
    avg64 = _block_diag_ones(RW_HD, 1.0 / RW_HD)
    ones128 = _block_diag_ones(DN_HD)
    o_rw, o_dn, o_s5, o_gate = 0, 3488, 3488 + 4128, 3488 + 4128 + 1024

    xt = x.reshape(t, D_MODEL)
    for i in range(depth):
        xt = _ffn(xt, row(ffn1_norm[i]), ffn1_w1[i].astype(bf16), ffn1_w3[i].astype(bf16), ffn1_w2[i].astype(bf16))

        wi = w_in[i]
        g_mix = row(mix_norm[i])
        w_rw = jnp.concatenate([wi[:, o_rw : o_rw + 3488], jnp.zeros((D_MODEL, RW_COLS - 3488), f32)], axis=1).astype(bf16)
        w_dn = wi[:, o_dn : o_dn + 4 * MIX].astype(bf16)
        w_dns = jnp.concatenate([wi[:, o_dn + 4 * MIX : o_dn + 4128], jnp.zeros((D_MODEL, 96), f32)], axis=1).astype(bf16)
        w_s5 = wi[:, o_s5 : o_s5 + MIX].astype(bf16)
        w_gate = wi[:, o_gate:].astype(bf16)
        c_rw = _norm_mm(xt, g_mix, w_rw, 512).reshape(bsz, seq, RW_COLS)
        c_dn = _norm_mm(xt, g_mix, w_dn, 512).reshape(bsz, seq, 4 * MIX)
        c_dns = _norm_mm(xt, g_mix, w_dns, 128)
        c_s5 = _norm_mm(xt, g_mix, w_s5, 512)
        sg = _norm_mm(xt, g_mix, w_gate, 512, act="sigmoid")

        mu = jnp.concatenate([rwkv_mu[i], jnp.zeros((RW_COLS - 3488,), f32)]).reshape(1, RW_COLS)
        zeros_blk = jnp.zeros((64, MIX), f32)
        w2cat = jnp.concatenate([
            jnp.concatenate([rwkv_w2[i, 0], zeros_blk], axis=1),
            jnp.concatenate([zeros_blk, rwkv_w2[i, 1]], axis=1)], axis=0).astype(bf16)
        a2cat = jnp.concatenate([
            jnp.concatenate([rwkv_a2[i, 0], zeros_blk], axis=1),
            jnp.concatenate([zeros_blk, rwkv_a2[i, 1]], axis=1)], axis=0).astype(bf16)
        g2pad = jnp.concatenate([rwkv_g2[i], jnp.zeros((256 - 160, MIX), f32)], axis=0).astype(bf16)
        prep = _rwkv_prep(c_rw, mu, row(rwkv_w0[i]), row(rwkv_a0[i]), w2cat, a2cat, g2pad,
                          row(rwkv_k_k[i]), row(rwkv_k_a[i]), row(rwkv_r_k[i]), ones64)
        r_, v_, an_, lwf_, lwb_, kf_, kb_, bf_, bb_, gate_, bonus_ = prep
        y_rw = _rwkv_scan(r_, v_, an_, lwf_, lwb_, kf_, kb_, bf_, bb_)
        ya = _rwkv_post(y_rw.reshape(t, MIX), bonus_.reshape(t, MIX), gate_.reshape(t, MIX),
                        row(rwkv_ln_w[i]), row(rwkv_ln_b[i]), avg64)

        conv8 = jnp.concatenate([dn_conv[i], jnp.zeros((8 - DN_CONV, 3 * MIX), f32)], axis=0)
        q_, k_, v2_ = _dn_prep(c_dn, conv8, ones128)
        n_dn = seq // DN_CHUNK
        gs = c_dns[:, :32].reshape(bsz, n_dn, DN_CHUNK, 4, DN_HEADS).transpose(0, 4, 1, 3, 2)
        gs = jnp.concatenate([gs, jnp.zeros((bsz, DN_HEADS, n_dn, 4, DN_CHUNK), f32)], axis=3)
        yb = _dn_scan(dn_a_log[i], dn_dt_bias[i], q_, k_, v2_, c_dn, gs, row(dn_norm[i])).reshape(t, MIX)

        n_s5 = seq // S5_CHUNK
        u_t = c_s5.reshape(bsz, n_s5, S5_CHUNK, S5_G, S5_CG).transpose(3, 1, 0, 2, 4).reshape(S5_G, n_s5 * bsz, S5_CHUNK * S5_CG)
        mf = _s5_matrices(s5_a_re[i, 0], s5_a_im[i, 0], s5_log_dt[i, 0], s5_b_re[i, 0], s5_b_im[i, 0], s5_c_re[i], s5_c_im[i], False)
        mb = _s5_matrices(s5_a_re[i, 1], s5_a_im[i, 1], s5_log_dt[i, 1], s5_b_re[i, 1], s5_b_im[i, 1], s5_c_re[i], s5_c_im[i], True)
        mats = (mf[0], mb[0], mf[1], mf[2], mb[1], mb[2], mf[3], mf[4], mb[3], mb[4], mf[5], mb[5])
        y_t = _s5_core(u_t, mats, bsz)
        y_s5 = y_t.reshape(S5_G, n_s5, bsz, S5_CHUNK, S5_CG).transpose(2, 1, 3, 0, 4).reshape(t, MIX)
        yc = _s5_post(y_s5, c_s5, row(s5_d[i]), s5_glu_w[i].astype(bf16), row(s5_glu_b[i]))

        xt = _merge(xt, ya, yb, yc, sg, w_branch_a[i].astype(bf16), w_branch_b[i].astype(bf16),
                    w_branch_c[i].astype(bf16), w_out[i].astype(bf16))
        xt = _ffn(xt, row(ffn2_norm[i]), ffn2_w1[i].astype(bf16), ffn2_w3[i].astype(bf16), ffn2_w2[i].astype(bf16))
        xt = _ple(xt, p[i].reshape(t, PLE_DIM), row(ple_norm[i]), ple_w_gate[i].astype(bf16), ple_w_proj[i].astype(bf16))
    return _final_norm(xt, row(final_norm)).reshape(bsz, seq, D_MODEL)
```

```python
import functools
import math

import jax
import jax.numpy as jnp
from jax import lax
from jax.experimental import pallas as pl
from jax.experimental.pallas import tpu as pltpu

f32 = jnp.float32
bf16 = jnp.bfloat16
HI = lax.Precision.HIGHEST

D_MODEL = 2048
D_FF = 5632
PLE_DIM = 256
NORM_EPS = 1e-6
MIX = 1024

RW_HEADS = 16
RW_HD = 64
RW_DECAY_SCALE = math.exp(-0.5)
RW_GN_EPS = 64e-5
RW_COLS = 3584
RW_CHUNK = 64

DN_HEADS = 8
DN_HD = 128
DN_CONV = 5
DN_CHUNK = 64

S5_G = 64
S5_CG = 16
S5_P = 64
S5_CHUNK = 16

VMEM_LIMIT = 56 * 1024 * 1024


def _cparams(n_axes):
    return pltpu.CompilerParams(
        dimension_semantics=("arbitrary",) * n_axes, vmem_limit_bytes=VMEM_LIMIT
    )


def _mm(a, b):
    return jnp.dot(a.astype(bf16), b.astype(bf16), preferred_element_type=f32)


def _bmm(a, b):
    return lax.dot_general(a.astype(bf16), b.astype(bf16), (((2,), (1,)), ((0,), (0,))), preferred_element_type=f32)


def _bmm_nt(a, b):
    return lax.dot_general(a.astype(bf16), b.astype(bf16), (((2,), (2,)), ((0,), (0,))), preferred_element_type=f32)


def _bmm_tn(a, b):
    return lax.dot_general(a.astype(bf16), b.astype(bf16), (((1,), (1,)), ((0,), (0,))), preferred_element_type=f32)


def _mm2(a, b01):
    hi = a.astype(bf16)
    lo = (a - hi.astype(f32)).astype(bf16)
    return jnp.dot(hi, b01, preferred_element_type=f32) + jnp.dot(lo, b01, preferred_element_type=f32)


def _mm3(m01, x):
    hi = x.astype(bf16)
    r1 = x - hi.astype(f32)
    mid = r1.astype(bf16)
    lo = (r1 - mid.astype(f32)).astype(bf16)
    dot = lambda p: jnp.dot(m01, p, preferred_element_type=f32)
    return dot(hi) + dot(mid) + dot(lo)


def _rms(x, g):
    return x * lax.rsqrt(jnp.mean(x * x, axis=-1, keepdims=True) + NORM_EPS) * g


def _sigmoid(x):
    return 1.0 / (1.0 + jnp.exp(-x))


def _silu(x):
    return x * _sigmoid(x)


def _softplus(x):
    return jnp.maximum(x, 0.0) + jnp.log(1.0 + jnp.exp(-jnp.abs(x)))


def _ffn_kernel(x_ref, g_ref, w1_ref, w3_ref, w2_ref, o_ref, h_ref):
    @pl.when(pl.program_id(1) == 0)
    def _():
        x = x_ref[...]
        h_ref[...] = _rms(x, g_ref[...]).astype(bf16)
        o_ref[...] = x

    h = h_ref[...]
    a = jnp.dot(h, w1_ref[...], preferred_element_type=f32)
    b = jnp.dot(h, w3_ref[...], preferred_element_type=f32)
    act = (0.5 * _silu(a) * b).astype(bf16)
    o_ref[...] += jnp.dot(act, w2_ref[...], preferred_element_type=f32)


def _ffn(x, g, w1, w3, w2, tm=512, tf=512):
    t = x.shape[0]
    tm = min(tm, t)
    return pl.pallas_call(
        _ffn_kernel,
        out_shape=jax.ShapeDtypeStruct((t, D_MODEL), f32),
        grid=(t // tm, D_FF // tf),
        in_specs=[
            pl.BlockSpec((tm, D_MODEL), lambda i, j: (i, 0)),
            pl.BlockSpec((1, D_MODEL), lambda i, j: (0, 0)),
            pl.BlockSpec((D_MODEL, tf), lambda i, j: (0, j)),
            pl.BlockSpec((D_MODEL, tf), lambda i, j: (0, j)),
            pl.BlockSpec((tf, D_MODEL), lambda i, j: (j, 0)),
        ],
        out_specs=pl.BlockSpec((tm, D_MODEL), lambda i, j: (i, 0)),
        scratch_shapes=[pltpu.VMEM((tm, D_MODEL), bf16)],
        compiler_params=_cparams(2),
        name="ffn",
    )(x, g, w1, w3, w2)


def _norm_mm_kernel(x_ref, g_ref, w_ref, o_ref, h_ref, *, act):
    @pl.when(pl.program_id(1) == 0)
    def _():
        h_ref[...] = _rms(x_ref[...], g_ref[...]).astype(bf16)

    y = jnp.dot(h_ref[...], w_ref[...], preferred_element_type=f32)
    if act == "sigmoid":
        y = _sigmoid(y)
    o_ref[...] = y.astype(o_ref.dtype)


def _norm_mm(x, g, w, tn, act=None, out_dtype=f32, tm=512):
    t = x.shape[0]
    n = w.shape[1]
    tm = min(tm, t)
    return pl.pallas_call(
        functools.partial(_norm_mm_kernel, act=act),
        out_shape=jax.ShapeDtypeStruct((t, n), out_dtype),
        grid=(t // tm, n // tn),
        in_specs=[
            pl.BlockSpec((tm, D_MODEL), lambda i, j: (i, 0)),
            pl.BlockSpec((1, D_MODEL), lambda i, j: (0, 0)),
            pl.BlockSpec((D_MODEL, tn), lambda i, j: (0, j)),
        ],
        out_specs=pl.BlockSpec((tm, tn), lambda i, j: (i, j)),
        scratch_shapes=[pltpu.VMEM((tm, D_MODEL), bf16)],
        compiler_params=_cparams(2),
        name="norm_mm",
    )(x, g, w)


def _merge_kernel(x_ref, ya_ref, yb_ref, yc_ref, sg_ref, wa_ref, wb_ref, wc_ref, wo_ref, o_ref):
    @pl.when(pl.program_id(1) == 0)
    def _():
        o_ref[...] = x_ref[...]

    sg = sg_ref[...]
    m = (
        sg[:, 0, :] * jnp.dot(ya_ref[...], wa_ref[...], preferred_element_type=f32)
        + sg[:, 1, :] * jnp.dot(yb_ref[...], wb_ref[...], preferred_element_type=f32)
        + sg[:, 2, :] * jnp.dot(yc_ref[...], wc_ref[...], preferred_element_type=f32)
    )
    o_ref[...] += jnp.dot(m.astype(bf16), wo_ref[...], preferred_element_type=f32)


def _merge(x, ya, yb, yc, sg, wa, wb, wc, wo, tm=512, tn=512):
    t = x.shape[0]
    tm = min(tm, t)
    sg3 = sg.reshape(t, 3, D_MODEL)
    return pl.pallas_call(
        _merge_kernel,
        out_shape=jax.ShapeDtypeStruct((t, D_MODEL), f32),
        grid=(t // tm, D_MODEL // tn),
        in_specs=[
            pl.BlockSpec((tm, D_MODEL), lambda i, j: (i, 0)),
            pl.BlockSpec((tm, MIX), lambda i, j: (i, 0)),
            pl.BlockSpec((tm, MIX), lambda i, j: (i, 0)),
            pl.BlockSpec((tm, MIX), lambda i, j: (i, 0)),
            pl.BlockSpec((tm, 3, tn), lambda i, j: (i, 0, j)),
            pl.BlockSpec((MIX, tn), lambda i, j: (0, j)),
            pl.BlockSpec((MIX, tn), lambda i, j: (0, j)),
            pl.BlockSpec((MIX, tn), lambda i, j: (0, j)),
            pl.BlockSpec((tn, D_MODEL), lambda i, j: (j, 0)),
        ],
        out_specs=pl.BlockSpec((tm, D_MODEL), lambda i, j: (i, 0)),
        compiler_params=_cparams(2),
        name="merge",
    )(x, ya, yb, yc, sg3, wa, wb, wc, wo)


def _ple_kernel(x_ref, xj_ref, p_ref, g_ref, wg_ref, wp_ref, o_ref, h_ref):
    @pl.when(pl.program_id(1) == 0)
    def _():
        h_ref[...] = _rms(x_ref[...], g_ref[...]).astype(bf16)

    gate = _sigmoid(jnp.dot(h_ref[...], wg_ref[...], preferred_element_type=f32))
    emb = jnp.dot(p_ref[...].astype(bf16), wp_ref[...], preferred_element_type=f32)
    o_ref[...] = xj_ref[...] + emb * gate


def _ple(x, p, g, wg, wp, tm=512, tn=512):
    t = x.shape[0]
    tm = min(tm, t)
    return pl.pallas_call(
        _ple_kernel,
        out_shape=jax.ShapeDtypeStruct((t, D_MODEL), f32),
        grid=(t // tm, D_MODEL // tn),
        in_specs=[
            pl.BlockSpec((tm, D_MODEL), lambda i, j: (i, 0)),
            pl.BlockSpec((tm, tn), lambda i, j: (i, j)),
            pl.BlockSpec((tm, PLE_DIM), lambda i, j: (i, 0)),
            pl.BlockSpec((1, D_MODEL), lambda i, j: (0, 0)),
            pl.BlockSpec((D_MODEL, tn), lambda i, j: (0, j)),
            pl.BlockSpec((PLE_DIM, tn), lambda i, j: (0, j)),
        ],
        out_specs=pl.BlockSpec((tm, tn), lambda i, j: (i, j)),
        scratch_shapes=[pltpu.VMEM((tm, D_MODEL), bf16)],
        compiler_params=_cparams(2),
        name="ple",
    )(x, x, p, g, wg, wp)


def _final_norm_kernel(x_ref, g_ref, o_ref):
    o_ref[...] = _rms(x_ref[...], g_ref[...])


def _final_norm(x, g, tm=512):
    t = x.shape[0]
    tm = min(tm, t)
    return pl.pallas_call(
        _final_norm_kernel,
        out_shape=jax.ShapeDtypeStruct((t, D_MODEL), f32),
        grid=(t // tm,),
        in_specs=[
            pl.BlockSpec((tm, D_MODEL), lambda i: (i, 0)),
            pl.BlockSpec((1, D_MODEL), lambda i: (0, 0)),
        ],
        out_specs=pl.BlockSpec((tm, D_MODEL), lambda i: (i, 0)),
        compiler_params=_cparams(1),
        name="final_norm",
    )(x, g)


def _halo(prev_ref, c_ref, next_ref, n_tiles):
    i = pl.program_id(1)
    prev = jnp.where(i == 0, 0.0, prev_ref[...])
    nxt = jnp.where(i == n_tiles - 1, 0.0, next_ref[...])
    return jnp.concatenate([prev, c_ref[...], nxt], axis=0)


def _shifted(ext, k, tl):
    n = ext.shape[0]
    return pltpu.roll(ext, (-k) % n, axis=0)[8 : 8 + tl]


def _iota2(c):
    return lax.broadcasted_iota(jnp.int32, (c, c), 0), lax.broadcasted_iota(jnp.int32, (c, c), 1)


def _by_dir(x, fn_f, fn_b):
    n = x.shape[0] // 2
    return jnp.concatenate([fn_f(x[:n]), fn_b(x[n:])], axis=0)


def _mask_dir(x, mask_f, mask_b):
    return _by_dir(x, lambda a: jnp.where(mask_f, a, 0.0), lambda a: jnp.where(mask_b, a, 0.0))


def _unit_tri_inverse(m, c):
    ii, jj = _iota2(c)
    same = lambda sh: (ii >> sh) == (jj >> sh)
    d = (ii == jj).astype(f32) + _mask_dir(m, same(1) & (ii > jj), same(1) & (jj > ii))
    sh = 1
    while (1 << sh) < c:
        hi_i = ((ii >> sh) & 1) == 1
        hi_j = ((jj >> sh) & 1) == 1
        ms = _mask_dir(m, same(sh + 1) & hi_i & ~hi_j, same(sh + 1) & hi_j & ~hi_i)
        d = d + _bmm(d, _bmm(ms, d))
        sh += 1
    return d


def _rwkv_prep_kernel(
    prev_ref, c_ref, next_ref, mu_ref, w0_ref, a0_ref, w2_ref, a2_ref, g2_ref, kk_ref, ka_ref, rk_ref, ones_ref,
    r_o, v_o, an_o, lwf_o, lwb_o, kf_o, kb_o, bf_o, bb_o, gate_o, bonus_o, *, tl, n_tiles
):
    ext = _halo(prev_ref, c_ref, next_ref, n_tiles)
    c = ext[8 : 8 + tl]
    cs = c + (0.5 * (_shifted(ext, -1, tl) + _shifted(ext, 1, tl)) - c) * mu_ref[...]
    r = cs[:, 0:MIX]
    k = cs[:, MIX : 2 * MIX]
    v = cs[:, 2 * MIX : 3 * MIX]
    lw_in = jnp.tanh(cs[:, 3 * MIX : 3 * MIX + 128])
    la_in = cs[:, 3 * MIX + 128 : 3 * MIX + 256]
    lg_in = _sigmoid(cs[:, 3 * MIX + 256 : 3 * MIX + 512])
    logw = -RW_DECAY_SCALE * _sigmoid(w0_ref[...] + _mm(lw_in, w2_ref[...]))
    a = _sigmoid(a0_ref[...] + _mm(la_in, a2_ref[...]))
    gate_o[...] = _mm(lg_in, g2_ref[...])
    ones = ones_ref[...]
    kx = k * kk_ref[...]
    kk = kx * lax.rsqrt(_mm2(kx * kx, ones) + 1e-6)
    a_f = a[:, 0:MIX]
    a_b = a[:, MIX : 2 * MIX]
    ka = ka_ref[...]
    k_f = k * (1.0 + (a_f - 1.0) * ka)
    k_b = k * (1.0 + (a_b - 1.0) * ka)
    r_o[...] = r
    v_o[...] = v
    an_o[...] = -kk
    lwf_o[...] = logw[:, 0:MIX]
    lwb_o[...] = logw[:, MIX : 2 * MIX]
    kf_o[...] = k_f
    kb_o[...] = k_b
    bf_o[...] = kk * a_f
    bb_o[...] = kk * a_b
    bonus_o[...] = _mm2(r * (k_f + k_b) * rk_ref[...], ones) * v


def _rwkv_prep(c, mu, w0, a0, w2, a2, g2, k_k, k_a, r_k, ones64, tl=128):
    bsz, seq, _ = c.shape
    tl = min(tl, seq)
    n_tiles = seq // tl
    nb = tl // 8
    row = lambda a: pl.BlockSpec(a.shape, lambda b, i: (0,) * a.ndim)
    out = jax.ShapeDtypeStruct((bsz, seq, MIX), f32)
    ospec = pl.BlockSpec((None, tl, MIX), lambda b, i: (b, i, 0))
    return pl.pallas_call(
        functools.partial(_rwkv_prep_kernel, tl=tl, n_tiles=n_tiles),
        out_shape=[out] * 11,
        grid=(bsz, n_tiles),
        in_specs=[
            pl.BlockSpec((None, 8, RW_COLS), lambda b, i: (b, jnp.maximum(i * nb - 1, 0), 0)),
            pl.BlockSpec((None, tl, RW_COLS), lambda b, i: (b, i, 0)),
            pl.BlockSpec((None, 8, RW_COLS), lambda b, i: (b, jnp.minimum((i + 1) * nb, seq // 8 - 1), 0)),
            row(mu), row(w0), row(a0), row(w2), row(a2), row(g2), row(k_k), row(k_a), row(r_k), row(ones64),
        ],
        out_specs=[ospec] * 11,
        compiler_params=_cparams(2),
        name="rwkv_prep",
    )(c, c, c, mu, w0, a0, w2, a2, g2, k_k, k_a, r_k, ones64)


def _rwkv_chunks(s, at, rt, bt, kt, v, bh, kh, e_tot):
    c = RW_CHUNK
    ii, jj = _iota2(c)
    ar = jnp.concatenate([at, rt], axis=1)
    rb = _bmm_nt(ar, bt)
    rk = _bmm_nt(ar, kt)
    a_ab = _mask_dir(rb[:, :c], jj < ii, jj > ii)
    a_ak = _mask_dir(rk[:, :c], jj < ii, jj > ii)
    p_b = _mask_dir(rb[:, c:], jj <= ii, jj >= ii)
    p_k = _mask_dir(rk[:, c:], jj <= ii, jj >= ii)
    tinv = _unit_tri_inverse(a_ab, c)
    w = _bmm(tinv, at)
    u0 = _bmm(tinv, _bmm(a_ak, v))
    ws = _bmm_nt(jnp.concatenate([w.astype(bf16), rt], axis=1), s)
    u = ws[:, :c] + u0
    y = ws[:, c:] + _bmm(p_b, u) + _bmm(p_k, v)
    s_new = s * e_tot + _bmm_tn(jnp.concatenate([u.astype(bf16), v], axis=1), jnp.concatenate([bh, kh], axis=1))
    return y, s_new


def _rwkv_scan_kernel(rf, vf, anf, lwf, kf, bf_, rb, vb, anb, lwb, kb, bb, yf_ref, yb_ref, s_ref):
    c = RW_CHUNK

    @pl.when(pl.program_id(1) == 0)
    def _():
        s_ref[...] = jnp.zeros_like(s_ref)

    ii, jj = _iota2(c)

    def side(r_ref, v_ref, an_ref, lw_ref, k_ref, b_ref, cum, last):
        lw = lw_ref[...]
        cl = _mm3(cum.astype(bf16), lw)
        e_neg = jnp.exp(-cl)
        cl_last = cl[last : last + 1, :]
        e_last = jnp.exp(cl_last - cl)
        b = b_ref[...]
        kd = k_ref[...]
        parts = (an_ref[...] * jnp.exp(cl - lw), r_ref[...] * jnp.exp(cl), b * e_neg, kd * e_neg, v_ref[...], b * e_last, kd * e_last)
        return [p.astype(bf16) for p in parts], jnp.exp(cl_last)

    pf, ef = side(rf, vf, anf, lwf, kf, bf_, jj <= ii, c - 1)
    pb, eb = side(rb, vb, anb, lwb, kb, bb, jj >= ii, 0)
    heads = lambda x: [x[:, h * RW_HD : (h + 1) * RW_HD] for h in range(RW_HEADS)]
    ops = [jnp.stack(heads(a) + heads(b), axis=0) for a, b in zip(pf, pb)]
    e_tot = jnp.stack(heads(ef) + heads(eb), axis=0)
    y, s_new = _rwkv_chunks(s_ref[...], *ops, e_tot)
    s_ref[...] = s_new
    for h in range(RW_HEADS):
        yf_ref[:, h * RW_HD : (h + 1) * RW_HD] = y[h]
        yb_ref[:, h * RW_HD : (h + 1) * RW_HD] = y[RW_HEADS + h]


def _rwkv_scan(r, v, an, lwf, lwb, kf, kb, bf_, bb):
    bsz, seq, _ = r.shape
    n_chunks = seq // RW_CHUNK
    fwd = pl.BlockSpec((None, RW_CHUNK, MIX), lambda b, i: (b, i, 0))
    bwd = pl.BlockSpec((None, RW_CHUNK, MIX), lambda b, i: (b, n_chunks - 1 - i, 0))
    out = jax.ShapeDtypeStruct((bsz, seq, MIX), f32)
    return pl.pallas_call(
        _rwkv_scan_kernel,
        out_shape=[out, out],
        grid=(bsz, n_chunks),
        in_specs=[fwd] * 6 + [bwd] * 6,
        out_specs=[fwd, bwd],
        scratch_shapes=[pltpu.VMEM((2 * RW_HEADS, RW_HD, RW_HD), f32)],
        compiler_params=_cparams(2),
        name="rwkv_scan",
    )(r, v, an, lwf, kf, bf_, r, v, an, lwb, kb, bb)


def _rwkv_post_kernel(yf_ref, yb_ref, bonus_ref, gate_ref, lnw_ref, lnb_ref, avg_ref, o_ref):
    y = yf_ref[...] + yb_ref[...]
    avg = avg_ref[...]
    mean = _mm2(y, avg)
    yc = y - mean
    var = _mm2(yc * yc, avg)
    yn = yc * lax.rsqrt(var + RW_GN_EPS) * lnw_ref[...] + lnb_ref[...]
    o_ref[...] = ((yn + bonus_ref[...]) * gate_ref[...]).astype(o_ref.dtype)


def _rwkv_post(yf, yb, bonus, gate, ln_w, ln_b, avg64, tm=512):
    t = yf.shape[0]
    tm = min(tm, t)
    tok = pl.BlockSpec((tm, MIX), lambda i: (i, 0))
    row = pl.BlockSpec((1, MIX), lambda i: (0, 0))
    return pl.pallas_call(
        _rwkv_post_kernel,
        out_shape=jax.ShapeDtypeStruct((t, MIX), bf16),
        grid=(t // tm,),
        in_specs=[tok, tok, tok, tok, row, row, pl.BlockSpec((MIX, MIX), lambda i: (0, 0))],
        out_specs=tok,
        compiler_params=_cparams(1),
        name="rwkv_post",
    )(yf, yb, bonus, gate, ln_w, ln_b, avg64)


def _dn_prep_kernel(prev_ref, c_ref, next_ref, cw_ref, ones_ref, q_o, k_o, v_o, *, tl, n_tiles):
    ext = _halo(prev_ref, c_ref, next_ref, n_tiles)
    cw = cw_ref[...]
    acc = None
    for j in range(DN_CONV):
        term = _shifted(ext, j - DN_CONV // 2, tl) * cw[j : j + 1, :]
        acc = term if acc is None else acc + term
    qkv = _silu(acc)
    ones = ones_ref[...]
    q = qkv[:, 0:MIX]
    k = qkv[:, MIX : 2 * MIX]
    q_o[...] = q * lax.rsqrt(_mm2(q * q, ones) + 1e-6) * (DN_HD ** -0.5)
    k_o[...] = k * lax.rsqrt(_mm2(k * k, ones) + 1e-6)
    v_o[...] = qkv[:, 2 * MIX : 3 * MIX]


def _dn_prep(c, conv_w8, ones128, tl=128):
    bsz, seq, _ = c.shape
    tl = min(tl, seq)
    n_tiles = seq // tl
    nb = tl // 8
    w = 3 * MIX
    out = jax.ShapeDtypeStruct((bsz, seq, MIX), f32)
    ospec = pl.BlockSpec((None, tl, MIX), lambda b, i: (b, i, 0))
    return pl.pallas_call(
        functools.partial(_dn_prep_kernel, tl=tl, n_tiles=n_tiles),
        out_shape=[out] * 3,
        grid=(bsz, n_tiles),
        in_specs=[
            pl.BlockSpec((None, 8, w), lambda b, i: (b, jnp.maximum(i * nb - 1, 0), 0)),
            pl.BlockSpec((None, tl, w), lambda b, i: (b, i, 0)),
            pl.BlockSpec((None, 8, w), lambda b, i: (b, jnp.minimum((i + 1) * nb, seq // 8 - 1), 0)),
            pl.BlockSpec((8, w), lambda b, i: (0, 0)),
            pl.BlockSpec((MIX, MIX), lambda b, i: (0, 0)),
        ],
        out_specs=[ospec] * 3,
        compiler_params=_cparams(2),
        name="dn_prep",
    )(c, c, c, conv_w8, ones128)


def _dn_chunks(s, q, k, v, beta, gcol, grow):
    c = DN_CHUNK
    ii, jj = _iota2(c)
    incl_f, incl_b = jj <= ii, jj >= ii
    diff = gcol[:, :, :c] - grow
    decay = _by_dir(
        diff,
        lambda a: jnp.where(incl_f, jnp.exp(jnp.where(incl_f, a, 0.0)), 0.0),
        lambda a: jnp.where(incl_b, jnp.exp(jnp.where(incl_b, a, 0.0)), 0.0),
    )
    kb = k * beta
    r = _bmm_nt(jnp.concatenate([kb, q], axis=1), k)
    nmat = _mask_dir(r[:, :c] * decay, jj < ii, jj > ii)
    qk = r[:, c:] * decay
    tinv = _unit_tri_inverse(-nmat, c)
    e_gc = jnp.exp(gcol)
    uw = _bmm(tinv, jnp.concatenate([v * beta, kb * e_gc], axis=2))
    u = uw[:, :, :DN_HD]
    w = uw[:, :, DN_HD:]
    wq = _bmm(jnp.concatenate([w, q * e_gc], axis=1), s)
    v_new = u - wq[:, :c]
    o = wq[:, c:] + _bmm(qk, v_new)
    gl = _by_dir(gcol, lambda a: a[:, c - 1 : c, :], lambda a: a[:, 0:1, :])
    s_new = s * jnp.exp(gl) + _bmm_tn(k * jnp.exp(gl - gcol), v_new)
    return o, s_new


def _dn_scan_kernel(qf, kf, vf, gf, qb, kb, vb, gb, alog_ref, dtb_ref, of_ref, ob_ref, s_ref):
    c = DN_CHUNK

    @pl.when(pl.program_id(1) == 0)
    def _():
        s_ref[...] = jnp.zeros_like(s_ref)

    ii, jj = _iota2(c)
    qs, ks, vs, betas, gcols, grows = [], [], [], [], [], []
    for d, (q_ref, k_ref, v_ref, g_ref, cum) in enumerate(((qf, kf, vf, gf, jj <= ii), (qb, kb, vb, gb, jj >= ii))):
        x = g_ref[...]
        beta = _sigmoid(x)
        g = -jnp.exp(alog_ref[...]) * _softplus(x + dtb_ref[...])
        gc = _mm3(cum.astype(bf16), g)
        gct = gc.T
        q, k, v = q_ref[...], k_ref[...], v_ref[...]
        for h in range(DN_HEADS):
            col = 2 * DN_HEADS + DN_HEADS * d + h
            bcol = DN_HEADS * d + h
            sl = slice(h * DN_HD, (h + 1) * DN_HD)
            qs.append(q[:, sl])
            ks.append(k[:, sl])
            vs.append(v[:, sl])
            betas.append(jnp.broadcast_to(beta[:, bcol : bcol + 1], (c, DN_HD)))
            gcols.append(jnp.broadcast_to(gc[:, col : col + 1], (c, DN_HD)))
            grows.append(gct[col : col + 1, :])
    st = lambda xs: jnp.stack(xs, axis=0)
    o, s_new = _dn_chunks(s_ref[...], st(qs), st(ks), st(vs), st(betas), st(gcols), st(grows))
    s_ref[...] = s_new
    for h in range(DN_HEADS):
        of_ref[:, h * DN_HD : (h + 1) * DN_HD] = o[h]
        ob_ref[:, h * DN_HD : (h + 1) * DN_HD] = o[DN_HEADS + h]


def _dn_scan(q, k, v, gates, alog_row, dtb_row):
    bsz, seq, _ = q.shape
    n_chunks = seq // DN_CHUNK
    fwd = lambda w: pl.BlockSpec((None, DN_CHUNK, w), lambda b, i: (b, i, 0))
    bwd = lambda w: pl.BlockSpec((None, DN_CHUNK, w), lambda b, i: (b, n_chunks - 1 - i, 0))
    row = pl.BlockSpec((1, 128), lambda b, i: (0, 0))
    out = jax.ShapeDtypeStruct((bsz, seq, MIX), f32)
    return pl.pallas_call(
        _dn_scan_kernel,
        out_shape=[out, out],
        grid=(bsz, n_chunks),
        in_specs=[fwd(MIX)] * 3 + [fwd(128)] + [bwd(MIX)] * 3 + [bwd(128)] + [row, row],
        out_specs=[fwd(MIX), bwd(MIX)],
        scratch_shapes=[pltpu.VMEM((2 * DN_HEADS, DN_HD, DN_HD), f32)],
        compiler_params=_cparams(2),
        name="dn_scan",
    )(q, k, v, gates, q, k, v, gates, alog_row, dtb_row)


def _dn_post_kernel(of_ref, ob_ref, z_ref, nw_ref, avg_ref, o_ref):
    o = of_ref[...] + ob_ref[...]
    o = o * lax.rsqrt(_mm2(o * o, avg_ref[...]) + NORM_EPS) * nw_ref[...]
    o_ref[...] = (o * _silu(z_ref[...])).astype(o_ref.dtype)


def _dn_post(of, ob, c_dn, norm_w, avg128, tm=512):
    t = of.shape[0]
    tm = min(tm, t)
    tok = pl.BlockSpec((tm, MIX), lambda i: (i, 0))
    return pl.pallas_call(
        _dn_post_kernel,
        out_shape=jax.ShapeDtypeStruct((t, MIX), bf16),
        grid=(t // tm,),
        in_specs=[tok, tok, pl.BlockSpec((tm, MIX), lambda i: (i, 3)),
                  pl.BlockSpec((1, MIX), lambda i: (0, 0)), pl.BlockSpec((MIX, MIX), lambda i: (0, 0))],
        out_specs=tok,
        compiler_params=_cparams(1),
        name="dn_post",
    )(of, ob, c_dn, norm_w, avg128)


def _s5_kernel(u_ref, tf_ref, tb_ref, frf_ref, fif_ref, frb_ref, fib_ref, erf_ref, eif_ref, erb_ref, eib_ref,
               pf_ref, pb_ref, y_ref, vr_ref, vi_ref, xr_ref, xi_ref, *, n_chunks, bsz):
    u = u_ref[...].astype(bf16)
    y = jnp.dot(u, tf_ref[...], preferred_element_type=f32) + jnp.dot(u, tb_ref[...], preferred_element_type=f32)
    for d, (fr, fi, er, ei, pw) in enumerate(
        ((frf_ref, fif_ref, erf_ref, eif_ref, pf_ref), (frb_ref, fib_ref, erb_ref, eib_ref, pb_ref))
    ):
        vr_ref[...] = jnp.dot(u, fr[...], preferred_element_type=f32)
        vi_ref[...] = jnp.dot(u, fi[...], preferred_element_type=f32)
        ar = pw[0:1, :]
        ai = pw[1:2, :]

        def body(n, carry, d=d, ar=ar, ai=ai):
            xr, xi = carry
            ci = n if d == 0 else n_chunks - 1 - n
            rows = pl.ds(pl.multiple_of(ci * bsz, bsz), bsz)
            xr_ref[rows, :] = xr
            xi_ref[rows, :] = xi
            return (ar * xr - ai * xi + vr_ref[rows, :], ar * xi + ai * xr + vi_ref[rows, :])

        zero = jnp.zeros((bsz, S5_P), f32)
        lax.fori_loop(0, n_chunks, body, (zero, zero))
        y = y + _mm(xr_ref[...], er[...]) + _mm(xi_ref[...], ei[...])
    y_ref[...] = y


def _s5_core(u_t, mats, bsz):
    g, m, w = u_t.shape
    n_chunks = m // bsz
    blk = lambda a: pl.BlockSpec((None,) + a.shape[1:], lambda i: (i,) + (0,) * (a.ndim - 1))
    return pl.pallas_call(
        functools.partial(_s5_kernel, n_chunks=n_chunks, bsz=bsz),
        out_shape=jax.ShapeDtypeStruct((g, m, w), f32),
        grid=(g,),
        in_specs=[blk(u_t)] + [blk(a) for a in mats],
        out_specs=blk(u_t),
        scratch_shapes=[pltpu.VMEM((m, S5_P), f32)] * 4,
        compiler_params=_cparams(1),
        name="s5_core",
    )(u_t, *mats)


def _s5_post_kernel(y_ref, u_ref, d_ref, w_ref, b_ref, o_ref):
    y = y_ref[...] + d_ref[...] * u_ref[...]
    y = 0.5 * y * (1.0 + jnp.tanh(math.sqrt(2.0 / math.pi) * (y + 0.044715 * (y * y * y))))
    o_ref[...] = (y * _sigmoid(_mm(y, w_ref[...]) + b_ref[...])).astype(o_ref.dtype)


def _s5_post(y, u, d_skip, glu_w, glu_b, tm=512):
    t = y.shape[0]
    tm = min(tm, t)
    tok = pl.BlockSpec((tm, MIX), lambda i: (i, 0))
    row = pl.BlockSpec((1, MIX), lambda i: (0, 0))
    return pl.pallas_call(
        _s5_post_kernel,
        out_shape=jax.ShapeDtypeStruct((t, MIX), bf16),
        grid=(t // tm,),
        in_specs=[tok, tok, row, pl.BlockSpec((MIX, MIX), lambda i: (0, 0)), row],
        out_specs=tok,
        compiler_params=_cparams(1),
        name="s5_post",
    )(y, u, d_skip, glu_w, glu_b)


def _s5_matrices(a_re, a_im, log_dt, b_re, b_im, c_re, c_im, reverse):
    c = S5_CHUNK
    dt = jnp.exp(log_dt)[:, None]
    mag = a_re * dt
    ph = a_im * dt

    def power(t):
        m = jnp.exp(mag[None] * t[:, None, None])
        return m * jnp.cos(ph[None] * t[:, None, None]), m * jnp.sin(ph[None] * t[:, None, None])

    a1r, a1i = power(jnp.ones((1,), f32))
    a1r, a1i = a1r[0], a1i[0]
    den = a_re * a_re + a_im * a_im
    qr = ((a1r - 1.0) * a_re + a1i * a_im) / den
    qi = (a1i * a_re - (a1r - 1.0) * a_im) / den
    bbr = qr[..., None] * b_re - qi[..., None] * b_im
    bbi = qr[..., None] * b_im + qi[..., None] * b_re
    steps = jnp.arange(c + 1, dtype=f32)
    pr, pi = power(steps)
    car = jnp.einsum("gcp,tgp->tgcp", c_re, pr, precision=HI) - jnp.einsum("gcp,tgp->tgcp", c_im, pi, precision=HI)
    cai = jnp.einsum("gcp,tgp->tgcp", c_re, pi, precision=HI) + jnp.einsum("gcp,tgp->tgcp", c_im, pr, precision=HI)
    kmat = jnp.einsum("tgcp,gpd->tgcd", car, bbr, precision=HI) - jnp.einsum("tgcp,gpd->tgcd", cai, bbi, precision=HI)
    jidx = jnp.arange(c)[:, None]
    iidx = jnp.arange(c)[None, :]
    lag = (jidx - iidx) if reverse else (iidx - jidx)
    valid = lag >= 0
    kt = kmat[jnp.clip(lag, 0, c)]
    kt = jnp.where(valid[:, :, None, None, None], kt, 0.0)
    tmat = kt.transpose(2, 0, 4, 1, 3).reshape(S5_G, c * S5_CG, c * S5_CG)
    fpow = jnp.arange(c) if reverse else (c - 1 - jnp.arange(c))
    fr = pr[fpow][:, :, :, None] * bbr[None] - pi[fpow][:, :, :, None] * bbi[None]
    fi = pr[fpow][:, :, :, None] * bbi[None] + pi[fpow][:, :, :, None] * bbr[None]
    fr = fr.transpose(1, 0, 3, 2).reshape(S5_G, c * S5_CG, S5_P)
    fi = fi.transpose(1, 0, 3, 2).reshape(S5_G, c * S5_CG, S5_P)
    epow = (c - jnp.arange(c)) if reverse else (jnp.arange(c) + 1)
    er = car[epow].transpose(1, 3, 0, 2).reshape(S5_G, S5_P, c * S5_CG)
    ei = (-cai[epow]).transpose(1, 3, 0, 2).reshape(S5_G, S5_P, c * S5_CG)
    pw = jnp.stack([pr[c], pi[c]], axis=1)
    pw = jnp.concatenate([pw, jnp.zeros((S5_G, 6, S5_P), f32)], axis=1)
    return tmat.astype(bf16), fr.astype(bf16), fi.astype(bf16), er.astype(bf16), ei.astype(bf16), pw


def _block_diag_ones(block, scale=1.0):
    idx = jnp.arange(MIX) // block
    return ((idx[:, None] == idx[None, :]).astype(f32) * scale).astype(bf16)


def kernel(x, p, ffn1_norm, ffn1_w1, ffn1_w3, ffn1_w2, mix_norm, w_in, rwkv_mu, rwkv_w0, rwkv_w2, rwkv_a0, rwkv_a2, rwkv_g2, rwkv_k_k, rwkv_k_a, rwkv_r_k, rwkv_ln_w, rwkv_ln_b, dn_conv, dn_a_log, dn_dt_bias, dn_norm, s5_a_re, s5_a_im, s5_log_dt, s5_b_re, s5_b_im, s5_c_re, s5_c_im, s5_d, s5_glu_w, s5_glu_b, w_branch_a, w_branch_b, w_branch_c, w_out, ffn2_norm, ffn2_w1, ffn2_w3, ffn2_w2, ple_norm, ple_w_gate, ple_w_proj, final_norm):
    bsz, seq, _ = x.shape
    t = bsz * seq
    depth = p.shape[0]
    row = lambda a: a.reshape(1, -1).astype(f32)
    ones64 = _block_diag_ones(RW_HD)
    avg64 = _block_diag_ones(RW_HD, 1.0 / RW_HD)
    ones128 = _block_diag_ones(DN_HD)
    avg128 = _block_diag_ones(DN_HD, 1.0 / DN_HD)
    o_rw, o_dn, o_s5, o_gate = 0, 3488, 3488 + 4128, 3488 + 4128 + 1024

    xt = x.reshape(t, D_MODEL)
    for i in range(depth):
        xt = _ffn(xt, row(ffn1_norm[i]), ffn1_w1[i].astype(bf16), ffn1_w3[i].astype(bf16), ffn1_w2[i].astype(bf16))

        wi = w_in[i]
        g_mix = row(mix_norm[i])
        w_rw = jnp.concatenate([wi[:, o_rw : o_rw + 3488], jnp.zeros((D_MODEL, RW_COLS - 3488), f32)], axis=1).astype(bf16)
        w_dn = wi[:, o_dn : o_dn + 4 * MIX].astype(bf16)
        w_dns = jnp.concatenate([wi[:, o_dn + 4 * MIX : o_dn + 4128], jnp.zeros((D_MODEL, 96), f32)], axis=1).astype(bf16)
        w_s5 = wi[:, o_s5 : o_s5 + MIX].astype(bf16)
        w_gate = wi[:, o_gate:].astype(bf16)
        c_rw = _norm_mm(xt, g_mix, w_rw, 512).reshape(bsz, seq, RW_COLS)
        c_dn = _norm_mm(xt, g_mix, w_dn, 512).reshape(bsz, seq, 4 * MIX)
        c_dns = _norm_mm(xt, g_mix, w_dns, 128)
        c_s5 = _norm_mm(xt, g_mix, w_s5, 512)
        sg = _norm_mm(xt, g_mix, w_gate, 512, act="sigmoid")

        mu = jnp.concatenate([rwkv_mu[i], jnp.zeros((RW_COLS - 3488,), f32)]).reshape(1, RW_COLS)
        zeros_blk = jnp.zeros((64, MIX), f32)
        w2cat = jnp.concatenate([
            jnp.concatenate([rwkv_w2[i, 0], zeros_blk], axis=1),
            jnp.concatenate([zeros_blk, rwkv_w2[i, 1]], axis=1)], axis=0).astype(bf16)
        a2cat = jnp.concatenate([
            jnp.concatenate([rwkv_a2[i, 0], zeros_blk], axis=1),
            jnp.concatenate([zeros_blk, rwkv_a2[i, 1]], axis=1)], axis=0).astype(bf16)
        g2pad = jnp.concatenate([rwkv_g2[i], jnp.zeros((256 - 160, MIX), f32)], axis=0).astype(bf16)
        prep = _rwkv_prep(c_rw, mu, row(rwkv_w0[i]), row(rwkv_a0[i]), w2cat, a2cat, g2pad,
                          row(rwkv_k_k[i]), row(rwkv_k_a[i]), row(rwkv_r_k[i]), ones64)
        r_, v_, an_, lwf_, lwb_, kf_, kb_, bf_, bb_, gate_, bonus_ = prep
        yf_rw, yb_rw = _rwkv_scan(r_, v_, an_, lwf_, lwb_, kf_, kb_, bf_, bb_)
        ya = _rwkv_post(yf_rw.reshape(t, MIX), yb_rw.reshape(t, MIX), bonus_.reshape(t, MIX), gate_.reshape(t, MIX),
                        row(rwkv_ln_w[i]), row(rwkv_ln_b[i]), avg64)

        conv8 = jnp.concatenate([dn_conv[i], jnp.zeros((8 - DN_CONV, 3 * MIX), f32)], axis=0)
        q_, k_, v2_ = _dn_prep(c_dn, conv8, ones128)
        pad16 = lambda a: jnp.concatenate([jnp.zeros((2 * DN_HEADS,), f32), a.reshape(-1), jnp.zeros((128 - 4 * DN_HEADS,), f32)]).reshape(1, 128)
        of_dn, ob_dn = _dn_scan(q_, k_, v2_, c_dns.reshape(bsz, seq, 128), pad16(dn_a_log[i]), pad16(dn_dt_bias[i]))
        yb = _dn_post(of_dn.reshape(t, MIX), ob_dn.reshape(t, MIX), c_dn.reshape(t, 4 * MIX), jnp.tile(dn_norm[i], DN_HEADS).reshape(1, MIX), avg128)

        n_s5 = seq // S5_CHUNK
        u_t = c_s5.reshape(bsz, n_s5, S5_CHUNK, S5_G, S5_CG).transpose(3, 1, 0, 2, 4).reshape(S5_G, n_s5 * bsz, S5_CHUNK * S5_CG)
        mf = _s5_matrices(s5_a_re[i, 0], s5_a_im[i, 0], s5_log_dt[i, 0], s5_b_re[i, 0], s5_b_im[i, 0], s5_c_re[i], s5_c_im[i], False)
        mb = _s5_matrices(s5_a_re[i, 1], s5_a_im[i, 1], s5_log_dt[i, 1], s5_b_re[i, 1], s5_b_im[i, 1], s5_c_re[i], s5_c_im[i], True)
        mats = (mf[0], mb[0], mf[1], mf[2], mb[1], mb[2], mf[3], mf[4], mb[3], mb[4], mf[5], mb[5])
        y_t = _s5_core(u_t, mats, bsz)
        y_s5 = y_t.reshape(S5_G, n_s5, bsz, S5_CHUNK, S5_CG).transpose(2, 1, 3, 0, 4).reshape(t, MIX)
        yc = _s5_post(y_s5, c_s5, row(s5_d[i]), s5_glu_w[i].astype(bf16), row(s5_glu_b[i]))

        xt = _merge(xt, ya, yb, yc, sg, w_branch_a[i].astype(bf16), w_branch_b[i].astype(bf16),
                    w_branch_c[i].astype(bf16), w_out[i].astype(bf16))
        xt = _ffn(xt, row(ffn2_norm[i]), ffn2_w1[i].astype(bf16), ffn2_w3[i].astype(bf16), ffn2_w2[i].astype(bf16))
        xt = _ple(xt, p[i].reshape(t, PLE_DIM), row(ple_norm[i]), ple_w_gate[i].astype(bf16), ple_w_proj[i].astype(bf16))
    return _final_norm(xt, row(final_norm)).reshape(bsz, seq, D_MODEL)
```

```python
import functools
import math

import jax
import jax.numpy as jnp
from jax import lax
from jax.experimental import pallas as pl
from jax.experimental.pallas import tpu as pltpu

f32 = jnp.float32
bf16 = jnp.bfloat16
HI = lax.Precision.HIGHEST

D_MODEL = 2048
D_FF = 5632
PLE_DIM = 256
NORM_EPS = 1e-6
MIX = 1024

RW_HEADS = 16
RW_HD = 64
RW_DECAY_SCALE = math.exp(-0.5)
RW_GN_EPS = 64e-5
RW_COLS = 3584
RW_CHUNK = 64

DN_HEADS = 8
DN_HD = 128
DN_CONV = 5
DN_CHUNK = 64

S5_G = 64
S5_CG = 16
S5_P = 64
S5_CHUNK = 16

PROJ_RW = 0
PROJ_DNS = 3584
PROJ_DN = 4096
PROJ_COLS = 8192

VMEM_LIMIT = 56 * 1024 * 1024


def _cparams(n_axes):
    return pltpu.CompilerParams(
        dimension_semantics=("arbitrary",) * n_axes, vmem_limit_bytes=VMEM_LIMIT
    )


def _mm(a, b):
    return jnp.dot(a.astype(bf16), b.astype(bf16), preferred_element_type=f32)


def _bmm(a, b):
    return lax.dot_general(a.astype(bf16), b.astype(bf16), (((2,), (1,)), ((0,), (0,))), preferred_element_type=f32)


def _bmm_nt(a, b):
    return lax.dot_general(a.astype(bf16), b.astype(bf16), (((2,), (2,)), ((0,), (0,))), preferred_element_type=f32)


def _bmm_tn(a, b):
    return lax.dot_general(a.astype(bf16), b.astype(bf16), (((1,), (1,)), ((0,), (0,))), preferred_element_type=f32)


def _mm2(a, b01):
    hi = a.astype(bf16)
    lo = (a - hi.astype(f32)).astype(bf16)
    return jnp.dot(hi, b01, preferred_element_type=f32) + jnp.dot(lo, b01, preferred_element_type=f32)


def _mm3(m01, x):
    hi = x.astype(bf16)
    r1 = x - hi.astype(f32)
    mid = r1.astype(bf16)
    lo = (r1 - mid.astype(f32)).astype(bf16)
    dot = lambda p: jnp.dot(m01, p, preferred_element_type=f32)
    return dot(hi) + dot(mid) + dot(lo)


def _rms(x, g):
    return x * lax.rsqrt(jnp.mean(x * x, axis=-1, keepdims=True) + NORM_EPS) * g


def _sigmoid(x):
    return 1.0 / (1.0 + jnp.exp(-x))


def _silu(x):
    return x * _sigmoid(x)


def _softplus(x):
    return jnp.maximum(x, 0.0) + jnp.log(1.0 + jnp.exp(-jnp.abs(x)))


def _ffn_kernel(x_ref, g_ref, w1_ref, w3_ref, w2_ref, o_ref, h_ref):
    @pl.when(pl.program_id(1) == 0)
    def _():
        x = x_ref[...]
        h_ref[...] = _rms(x, g_ref[...]).astype(bf16)
        o_ref[...] = x

    h = h_ref[...]
    a = jnp.dot(h, w1_ref[...], preferred_element_type=f32)
    b = jnp.dot(h, w3_ref[...], preferred_element_type=f32)
    act = (0.5 * _silu(a) * b).astype(bf16)
    o_ref[...] += jnp.dot(act, w2_ref[...], preferred_element_type=f32)


def _ffn(x, g, w1, w3, w2, tm=512, tf=512):
    t = x.shape[0]
    tm = min(tm, t)
    return pl.pallas_call(
        _ffn_kernel,
        out_shape=jax.ShapeDtypeStruct((t, D_MODEL), f32),
        grid=(t // tm, D_FF // tf),
        in_specs=[
            pl.BlockSpec((tm, D_MODEL), lambda i, j: (i, 0)),
            pl.BlockSpec((1, D_MODEL), lambda i, j: (0, 0)),
            pl.BlockSpec((D_MODEL, tf), lambda i, j: (0, j)),
            pl.BlockSpec((D_MODEL, tf), lambda i, j: (0, j)),
            pl.BlockSpec((tf, D_MODEL), lambda i, j: (j, 0)),
        ],
        out_specs=pl.BlockSpec((tm, D_MODEL), lambda i, j: (i, 0)),
        scratch_shapes=[pltpu.VMEM((tm, D_MODEL), bf16)],
        compiler_params=_cparams(2),
        name="ffn",
    )(x, g, w1, w3, w2)


def _norm_mm_kernel(x_ref, g_ref, w_ref, o_ref, h_ref):
    @pl.when(pl.program_id(1) == 0)
    def _():
        h_ref[...] = _rms(x_ref[...], g_ref[...]).astype(bf16)

    o_ref[...] = jnp.dot(h_ref[...], w_ref[...], preferred_element_type=f32)


def _norm_mm(x, g, w, tm=1024, tn=512):
    t = x.shape[0]
    n = w.shape[1]
    tm = min(tm, t)
    return pl.pallas_call(
        _norm_mm_kernel,
        out_shape=jax.ShapeDtypeStruct((t, n), f32),
        grid=(t // tm, n // tn),
        in_specs=[
            pl.BlockSpec((tm, D_MODEL), lambda i, j: (i, 0)),
            pl.BlockSpec((1, D_MODEL), lambda i, j: (0, 0)),
            pl.BlockSpec((D_MODEL, tn), lambda i, j: (0, j)),
        ],
        out_specs=pl.BlockSpec((tm, tn), lambda i, j: (i, j)),
        scratch_shapes=[pltpu.VMEM((tm, D_MODEL), bf16)],
        compiler_params=_cparams(2),
        name="norm_mm",
    )(x, g, w)


def _merge_kernel(x_ref, g_ref, ya_ref, yb_ref, yc_ref, wga_ref, wgb_ref, wgc_ref, wa_ref, wb_ref, wc_ref, wo_ref, o_ref, h_ref):
    @pl.when(pl.program_id(1) == 0)
    def _():
        x = x_ref[...]
        h_ref[...] = _rms(x, g_ref[...]).astype(bf16)
        o_ref[...] = x

    h = h_ref[...]
    m = None
    for y_ref, wg_ref, w_ref in ((ya_ref, wga_ref, wa_ref), (yb_ref, wgb_ref, wb_ref), (yc_ref, wgc_ref, wc_ref)):
        gate = _sigmoid(jnp.dot(h, wg_ref[...], preferred_element_type=f32))
        term = gate * jnp.dot(y_ref[...], w_ref[...], preferred_element_type=f32)
        m = term if m is None else m + term
    o_ref[...] += jnp.dot(m.astype(bf16), wo_ref[...], preferred_element_type=f32)


def _merge(x, g, ya, yb, yc, w_gate, wa, wb, wc, wo, tm=512, tn=512):
    t = x.shape[0]
    tm = min(tm, t)
    nj = D_MODEL // tn
    tok = pl.BlockSpec((tm, MIX), lambda i, j: (i, 0))
    gate_w = lambda b: pl.BlockSpec((D_MODEL, tn), lambda i, j: (0, b * nj + j))
    br_w = pl.BlockSpec((MIX, tn), lambda i, j: (0, j))
    return pl.pallas_call(
        _merge_kernel,
        out_shape=jax.ShapeDtypeStruct((t, D_MODEL), f32),
        grid=(t // tm, nj),
        in_specs=[
            pl.BlockSpec((tm, D_MODEL), lambda i, j: (i, 0)),
            pl.BlockSpec((1, D_MODEL), lambda i, j: (0, 0)),
            tok, tok, tok,
            gate_w(0), gate_w(1), gate_w(2),
            br_w, br_w, br_w,
            pl.BlockSpec((tn, D_MODEL), lambda i, j: (j, 0)),
        ],
        out_specs=pl.BlockSpec((tm, D_MODEL), lambda i, j: (i, 0)),
        scratch_shapes=[pltpu.VMEM((tm, D_MODEL), bf16)],
        compiler_params=_cparams(2),
        name="merge",
    )(x, g, ya, yb, yc, w_gate, w_gate, w_gate, wa, wb, wc, wo)


def _ple_kernel(x_ref, xj_ref, p_ref, g_ref, wg_ref, wp_ref, o_ref, h_ref):
    @pl.when(pl.program_id(1) == 0)
    def _():
        h_ref[...] = _rms(x_ref[...], g_ref[...]).astype(bf16)

    gate = _sigmoid(jnp.dot(h_ref[...], wg_ref[...], preferred_element_type=f32))
    emb = jnp.dot(p_ref[...].astype(bf16), wp_ref[...], preferred_element_type=f32)
    o_ref[...] = xj_ref[...] + emb * gate


def _ple(x, p, g, wg, wp, tm=512, tn=512):
    t = x.shape[0]
    tm = min(tm, t)
    return pl.pallas_call(
        _ple_kernel,
        out_shape=jax.ShapeDtypeStruct((t, D_MODEL), f32),
        grid=(t // tm, D_MODEL // tn),
        in_specs=[
            pl.BlockSpec((tm, D_MODEL), lambda i, j: (i, 0)),
            pl.BlockSpec((tm, tn), lambda i, j: (i, j)),
            pl.BlockSpec((tm, PLE_DIM), lambda i, j: (i, 0)),
            pl.BlockSpec((1, D_MODEL), lambda i, j: (0, 0)),
            pl.BlockSpec((D_MODEL, tn), lambda i, j: (0, j)),
            pl.BlockSpec((PLE_DIM, tn), lambda i, j: (0, j)),
        ],
        out_specs=pl.BlockSpec((tm, tn), lambda i, j: (i, j)),
        scratch_shapes=[pltpu.VMEM((tm, D_MODEL), bf16)],
        compiler_params=_cparams(2),
        name="ple",
    )(x, x, p, g, wg, wp)


def _final_norm_kernel(x_ref, g_ref, o_ref):
    o_ref[...] = _rms(x_ref[...], g_ref[...])


def _final_norm(x, g, tm=512):
    t = x.shape[0]
    tm = min(tm, t)
    return pl.pallas_call(
        _final_norm_kernel,
        out_shape=jax.ShapeDtypeStruct((t, D_MODEL), f32),
        grid=(t // tm,),
        in_specs=[
            pl.BlockSpec((tm, D_MODEL), lambda i: (i, 0)),
            pl.BlockSpec((1, D_MODEL), lambda i: (0, 0)),
        ],
        out_specs=pl.BlockSpec((tm, D_MODEL), lambda i: (i, 0)),
        compiler_params=_cparams(1),
        name="final_norm",
    )(x, g)


def _halo(prev_ref, c_ref, next_ref, n_tiles):
    i = pl.program_id(1)
    prev = jnp.where(i == 0, 0.0, prev_ref[...])
    nxt = jnp.where(i == n_tiles - 1, 0.0, next_ref[...])
    return jnp.concatenate([prev, c_ref[...], nxt], axis=0)


def _shifted(ext, k, tl):
    n = ext.shape[0]
    return pltpu.roll(ext, (-k) % n, axis=0)[8 : 8 + tl]


def _iota2(c):
    return lax.broadcasted_iota(jnp.int32, (c, c), 0), lax.broadcasted_iota(jnp.int32, (c, c), 1)


def _by_dir(x, fn_f, fn_b):
    n = x.shape[0] // 2
    return jnp.concatenate([fn_f(x[:n]), fn_b(x[n:])], axis=0)


def _mask_dir(x, mask_f, mask_b):
    return _by_dir(x, lambda a: jnp.where(mask_f, a, 0.0), lambda a: jnp.where(mask_b, a, 0.0))


def _unit_tri_inverse(m, c):
    ii, jj = _iota2(c)
    same = lambda sh: (ii >> sh) == (jj >> sh)
    d = (ii == jj).astype(f32) + _mask_dir(m, same(1) & (ii > jj), same(1) & (jj > ii))
    sh = 1
    while (1 << sh) < c:
        hi_i = ((ii >> sh) & 1) == 1
        hi_j = ((jj >> sh) & 1) == 1
        ms = _mask_dir(m, same(sh + 1) & hi_i & ~hi_j, same(sh + 1) & hi_j & ~hi_i)
        d = d + _bmm(d, _bmm(ms, d))
        sh += 1
    return d


def _rwkv_prep_kernel(
    prev_ref, c_ref, next_ref, mu_ref, w0_ref, a0_ref, w2_ref, a2_ref, g2_ref, kk_ref, ka_ref, rk_ref, ones_ref,
    r_o, v_o, an_o, lwf_o, lwb_o, kf_o, kb_o, bf_o, bb_o, gate_o, bonus_o, *, tl, n_tiles
):
    ext = _halo(prev_ref, c_ref, next_ref, n_tiles)
    c = ext[8 : 8 + tl]
    cs = c + (0.5 * (_shifted(ext, -1, tl) + _shifted(ext, 1, tl)) - c) * mu_ref[...]
    r = cs[:, 0:MIX]
    k = cs[:, MIX : 2 * MIX]
    v = cs[:, 2 * MIX : 3 * MIX]
    lw_in = jnp.tanh(cs[:, 3 * MIX : 3 * MIX + 128])
    la_in = cs[:, 3 * MIX + 128 : 3 * MIX + 256]
    lg_in = _sigmoid(cs[:, 3 * MIX + 256 : 3 * MIX + 512])
    logw = -RW_DECAY_SCALE * _sigmoid(w0_ref[...] + _mm(lw_in, w2_ref[...]))
    a = _sigmoid(a0_ref[...] + _mm(la_in, a2_ref[...]))
    gate_o[...] = _mm(lg_in, g2_ref[...])
    ones = ones_ref[...]
    kx = k * kk_ref[...]
    kk = kx * lax.rsqrt(_mm2(kx * kx, ones) + 1e-6)
    a_f = a[:, 0:MIX]
    a_b = a[:, MIX : 2 * MIX]
    ka = ka_ref[...]
    k_f = k * (1.0 + (a_f - 1.0) * ka)
    k_b = k * (1.0 + (a_b - 1.0) * ka)
    r_o[...] = r
    v_o[...] = v
    an_o[...] = -kk
    lwf_o[...] = logw[:, 0:MIX]
    lwb_o[...] = logw[:, MIX : 2 * MIX]
    kf_o[...] = k_f
    kb_o[...] = k_b
    bf_o[...] = kk * a_f
    bb_o[...] = kk * a_b
    bonus_o[...] = _mm2(r * (k_f + k_b) * rk_ref[...], ones) * v


def _rwkv_prep(c, mu, w0, a0, w2, a2, g2, k_k, k_a, r_k, ones64, tl=128):
    bsz, seq, _ = c.shape
    tl = min(tl, seq)
    n_tiles = seq // tl
    nb = tl // 8
    row = lambda a: pl.BlockSpec(a.shape, lambda b, i: (0,) * a.ndim)
    out = jax.ShapeDtypeStruct((bsz, seq, MIX), f32)
    ospec = pl.BlockSpec((None, tl, MIX), lambda b, i: (b, i, 0))
    return pl.pallas_call(
        functools.partial(_rwkv_prep_kernel, tl=tl, n_tiles=n_tiles),
        out_shape=[out] * 11,
        grid=(bsz, n_tiles),
        in_specs=[
            pl.BlockSpec((None, 8, RW_COLS), lambda b, i: (b, jnp.maximum(i * nb - 1, 0), 0)),
            pl.BlockSpec((None, tl, RW_COLS), lambda b, i: (b, i, 0)),
            pl.BlockSpec((None, 8, RW_COLS), lambda b, i: (b, jnp.minimum((i + 1) * nb, seq // 8 - 1), 0)),
            row(mu), row(w0), row(a0), row(w2), row(a2), row(g2), row(k_k), row(k_a), row(r_k), row(ones64),
        ],
        out_specs=[ospec] * 11,
        compiler_params=_cparams(2),
        name="rwkv_prep",
    )(c, c, c, mu, w0, a0, w2, a2, g2, k_k, k_a, r_k, ones64)


def _rwkv_chunks(s, at, rt, bt, kt, v, bh, kh, e_tot):
    c = RW_CHUNK
    ii, jj = _iota2(c)
    ar = jnp.concatenate([at, rt], axis=1)
    rb = _bmm_nt(ar, bt)
    rk = _bmm_nt(ar, kt)
    a_ab = _mask_dir(rb[:, :c], jj < ii, jj > ii)
    a_ak = _mask_dir(rk[:, :c], jj < ii, jj > ii)
    p_b = _mask_dir(rb[:, c:], jj <= ii, jj >= ii)
    p_k = _mask_dir(rk[:, c:], jj <= ii, jj >= ii)
    tinv = _unit_tri_inverse(a_ab, c)
    w = _bmm(tinv, at)
    u0 = _bmm(tinv, _bmm(a_ak, v))
    ws = _bmm_nt(jnp.concatenate([w.astype(bf16), rt], axis=1), s)
    u = ws[:, :c] + u0
    y = ws[:, c:] + _bmm(p_b, u) + _bmm(p_k, v)
    s_new = s * e_tot + _bmm_tn(jnp.concatenate([u.astype(bf16), v], axis=1), jnp.concatenate([bh, kh], axis=1))
    return y, s_new


def _rwkv_scan_kernel(rf, vf, anf, lwf, kf, bf_, rb, vb, anb, lwb, kb, bb, yf_ref, yb_ref, s_ref):
    c = RW_CHUNK

    @pl.when(pl.program_id(1) == 0)
    def _():
        s_ref[...] = jnp.zeros_like(s_ref)

    ii, jj = _iota2(c)

    def side(r_ref, v_ref, an_ref, lw_ref, k_ref, b_ref, cum, last):
        lw = lw_ref[...]
        cl = _mm3(cum.astype(bf16), lw)
        e_neg = jnp.exp(-cl)
        cl_last = cl[last : last + 1, :]
        e_last = jnp.exp(cl_last - cl)
        b = b_ref[...]
        kd = k_ref[...]
        parts = (an_ref[...] * jnp.exp(cl - lw), r_ref[...] * jnp.exp(cl), b * e_neg, kd * e_neg, v_ref[...], b * e_last, kd * e_last)
        return [p.astype(bf16) for p in parts], jnp.exp(cl_last)

    pf, ef = side(rf, vf, anf, lwf, kf, bf_, jj <= ii, c - 1)
    pb, eb = side(rb, vb, anb, lwb, kb, bb, jj >= ii, 0)
    heads = lambda x: [x[:, h * RW_HD : (h + 1) * RW_HD] for h in range(RW_HEADS)]
    ops = [jnp.stack(heads(a) + heads(b), axis=0) for a, b in zip(pf, pb)]
    e_tot = jnp.stack(heads(ef) + heads(eb), axis=0)
    y, s_new = _rwkv_chunks(s_ref[...], *ops, e_tot)
    s_ref[...] = s_new
    for h in range(RW_HEADS):
        yf_ref[:, h * RW_HD : (h + 1) * RW_HD] = y[h]
        yb_ref[:, h * RW_HD : (h + 1) * RW_HD] = y[RW_HEADS + h]


def _rwkv_scan(r, v, an, lwf, lwb, kf, kb, bf_, bb):
    bsz, seq, _ = r.shape
    n_chunks = seq // RW_CHUNK
    fwd = pl.BlockSpec((None, RW_CHUNK, MIX), lambda b, i: (b, i, 0))
    bwd = pl.BlockSpec((None, RW_CHUNK, MIX), lambda b, i: (b, n_chunks - 1 - i, 0))
    out = jax.ShapeDtypeStruct((bsz, seq, MIX), f32)
    return pl.pallas_call(
        _rwkv_scan_kernel,
        out_shape=[out, out],
        grid=(bsz, n_chunks),
        in_specs=[fwd] * 6 + [bwd] * 6,
        out_specs=[fwd, bwd],
        scratch_shapes=[pltpu.VMEM((2 * RW_HEADS, RW_HD, RW_HD), f32)],
        compiler_params=_cparams(2),
        name="rwkv_scan",
    )(r, v, an, lwf, kf, bf_, r, v, an, lwb, kb, bb)


def _rwkv_post_kernel(yf_ref, yb_ref, bonus_ref, gate_ref, lnw_ref, lnb_ref, avg_ref, o_ref):
    y = yf_ref[...] + yb_ref[...]
    avg = avg_ref[...]
    mean = _mm2(y, avg)
    yc = y - mean
    var = _mm2(yc * yc, avg)
    yn = yc * lax.rsqrt(var + RW_GN_EPS) * lnw_ref[...] + lnb_ref[...]
    o_ref[...] = ((yn + bonus_ref[...]) * gate_ref[...]).astype(o_ref.dtype)


def _rwkv_post(yf, yb, bonus, gate, ln_w, ln_b, avg64, tm=512):
    t = yf.shape[0]
    tm = min(tm, t)
    tok = pl.BlockSpec((tm, MIX), lambda i: (i, 0))
    row = pl.BlockSpec((1, MIX), lambda i: (0, 0))
    return pl.pallas_call(
        _rwkv_post_kernel,
        out_shape=jax.ShapeDtypeStruct((t, MIX), bf16),
        grid=(t // tm,),
        in_specs=[tok, tok, tok, tok, row, row, pl.BlockSpec((MIX, MIX), lambda i: (0, 0))],
        out_specs=tok,
        compiler_params=_cparams(1),
        name="rwkv_post",
    )(yf, yb, bonus, gate, ln_w, ln_b, avg64)


def _dn_prep_kernel(*refs, tl, n_tiles):
    halos, (cw_ref, ones_ref, q_o, k_o, v_o) = refs[:9], refs[9:]
    cw = cw_ref[...]
    parts = []
    for p in range(3):
        ext = _halo(*halos[3 * p : 3 * p + 3], n_tiles)
        acc = None
        for j in range(DN_CONV):
            term = _shifted(ext, j - DN_CONV // 2, tl) * cw[j : j + 1, p * MIX : (p + 1) * MIX]
            acc = term if acc is None else acc + term
        parts.append(_silu(acc))
    q, k, v = parts
    ones = ones_ref[...]
    q_o[...] = q * lax.rsqrt(_mm2(q * q, ones) + 1e-6) * (DN_HD ** -0.5)
    k_o[...] = k * lax.rsqrt(_mm2(k * k, ones) + 1e-6)
    v_o[...] = v


def _dn_prep(c, conv_w8, ones128, tl=128):
    bsz, seq, _ = c.shape
    tl = min(tl, seq)
    n_tiles = seq // tl
    nb = tl // 8
    out = jax.ShapeDtypeStruct((bsz, seq, MIX), f32)
    ospec = pl.BlockSpec((None, tl, MIX), lambda b, i: (b, i, 0))
    halo_specs = []
    for p in range(3):
        col = PROJ_DN // MIX + p
        halo_specs += [
            pl.BlockSpec((None, 8, MIX), lambda b, i, col=col: (b, jnp.maximum(i * nb - 1, 0), col)),
            pl.BlockSpec((None, tl, MIX), lambda b, i, col=col: (b, i, col)),
            pl.BlockSpec((None, 8, MIX), lambda b, i, col=col: (b, jnp.minimum((i + 1) * nb, seq // 8 - 1), col)),
        ]
    return pl.pallas_call(
        functools.partial(_dn_prep_kernel, tl=tl, n_tiles=n_tiles),
        out_shape=[out] * 3,
        grid=(bsz, n_tiles),
        in_specs=halo_specs + [
            pl.BlockSpec((8, 3 * MIX), lambda b, i: (0, 0)),
            pl.BlockSpec((MIX, MIX), lambda b, i: (0, 0)),
        ],
        out_specs=[ospec] * 3,
        compiler_params=_cparams(2),
        name="dn_prep",
    )(*([c] * 9), conv_w8, ones128)


def _dn_chunks(s, q, k, v, beta, gcol, grow):
    c = DN_CHUNK
    ii, jj = _iota2(c)
    incl_f, incl_b = jj <= ii, jj >= ii
    diff = gcol[:, :, :c] - grow
    decay = _by_dir(
        diff,
        lambda a: jnp.where(incl_f, jnp.exp(jnp.where(incl_f, a, 0.0)), 0.0),
        lambda a: jnp.where(incl_b, jnp.exp(jnp.where(incl_b, a, 0.0)), 0.0),
    )
    kb = k * beta
    r = _bmm_nt(jnp.concatenate([kb, q], axis=1), k)
    nmat = _mask_dir(r[:, :c] * decay, jj < ii, jj > ii)
    qk = r[:, c:] * decay
    tinv = _unit_tri_inverse(-nmat, c)
    e_gc = jnp.exp(gcol)
    uw = _bmm(tinv, jnp.concatenate([v * beta, kb * e_gc], axis=2))
    u = uw[:, :, :DN_HD]
    w = uw[:, :, DN_HD:]
    wq = _bmm(jnp.concatenate([w, q * e_gc], axis=1), s)
    v_new = u - wq[:, :c]
    o = wq[:, c:] + _bmm(qk, v_new)
    gl = _by_dir(gcol, lambda a: a[:, c - 1 : c, :], lambda a: a[:, 0:1, :])
    s_new = s * jnp.exp(gl) + _bmm_tn(k * jnp.exp(gl - gcol), v_new)
    return o, s_new


def _dn_scan_kernel(qf, kf, vf, gf, qb, kb, vb, gb, alog_ref, dtb_ref, of_ref, ob_ref, s_ref):
    c = DN_CHUNK

    @pl.when(pl.program_id(1) == 0)
    def _():
        s_ref[...] = jnp.zeros_like(s_ref)

    ii, jj = _iota2(c)
    qs, ks, vs, betas, gcols, grows = [], [], [], [], [], []
    for d, (q_ref, k_ref, v_ref, g_ref, cum) in enumerate(((qf, kf, vf, gf, jj <= ii), (qb, kb, vb, gb, jj >= ii))):
        x = g_ref[...]
        beta = _sigmoid(x)
        g = -jnp.exp(alog_ref[...]) * _softplus(x + dtb_ref[...])
        gc = _mm3(cum.astype(bf16), g)
        gct = gc.T
        q, k, v = q_ref[...], k_ref[...], v_ref[...]
        for h in range(DN_HEADS):
            col = 2 * DN_HEADS + DN_HEADS * d + h
            bcol = DN_HEADS * d + h
            sl = slice(h * DN_HD, (h + 1) * DN_HD)
            qs.append(q[:, sl])
            ks.append(k[:, sl])
            vs.append(v[:, sl])
            betas.append(jnp.broadcast_to(beta[:, bcol : bcol + 1], (c, DN_HD)))
            gcols.append(jnp.broadcast_to(gc[:, col : col + 1], (c, DN_HD)))
            grows.append(gct[col : col + 1, :])
    st = lambda xs: jnp.stack(xs, axis=0)
    o, s_new = _dn_chunks(s_ref[...], st(qs), st(ks), st(vs), st(betas), st(gcols), st(grows))
    s_ref[...] = s_new
    for h in range(DN_HEADS):
        of_ref[:, h * DN_HD : (h + 1) * DN_HD] = o[h]
        ob_ref[:, h * DN_HD : (h + 1) * DN_HD] = o[DN_HEADS + h]


def _dn_scan(q, k, v, gates, alog_row, dtb_row):
    bsz, seq, _ = q.shape
    n_chunks = seq // DN_CHUNK
    fwd = lambda w, col=0: pl.BlockSpec((None, DN_CHUNK, w), lambda b, i: (b, i, col))
    bwd = lambda w, col=0: pl.BlockSpec((None, DN_CHUNK, w), lambda b, i: (b, n_chunks - 1 - i, col))
    gcol = PROJ_DNS // 128
    row = pl.BlockSpec((1, 128), lambda b, i: (0, 0))
    out = jax.ShapeDtypeStruct((bsz, seq, MIX), f32)
    return pl.pallas_call(
        _dn_scan_kernel,
        out_shape=[out, out],
        grid=(bsz, n_chunks),
        in_specs=[fwd(MIX)] * 3 + [fwd(128, gcol)] + [bwd(MIX)] * 3 + [bwd(128, gcol)] + [row, row],
        out_specs=[fwd(MIX), bwd(MIX)],
        scratch_shapes=[pltpu.VMEM((2 * DN_HEADS, DN_HD, DN_HD), f32)],
        compiler_params=_cparams(2),
        name="dn_scan",
    )(q, k, v, gates, q, k, v, gates, alog_row, dtb_row)


def _dn_post_kernel(of_ref, ob_ref, z_ref, nw_ref, avg_ref, o_ref):
    o = of_ref[...] + ob_ref[...]
    o = o * lax.rsqrt(_mm2(o * o, avg_ref[...]) + NORM_EPS) * nw_ref[...]
    o_ref[...] = (o * _silu(z_ref[...])).astype(o_ref.dtype)


def _dn_post(of, ob, c_dn, norm_w, avg128, tm=512):
    t = of.shape[0]
    tm = min(tm, t)
    tok = pl.BlockSpec((tm, MIX), lambda i: (i, 0))
    return pl.pallas_call(
        _dn_post_kernel,
        out_shape=jax.ShapeDtypeStruct((t, MIX), bf16),
        grid=(t // tm,),
        in_specs=[tok, tok, pl.BlockSpec((tm, MIX), lambda i: (i, PROJ_DN // MIX + 3)),
                  pl.BlockSpec((1, MIX), lambda i: (0, 0)), pl.BlockSpec((MIX, MIX), lambda i: (0, 0))],
        out_specs=tok,
        compiler_params=_cparams(1),
        name="dn_post",
    )(of, ob, c_dn, norm_w, avg128)


S5_SG = 8
S5_NSG = S5_G // S5_SG
S5_ST = S5_SG * S5_P
S5_W = S5_CHUNK * 128
S5_TOK = 512


def _s5_proj_kernel(x_ref, g_ref, w_ref, o_ref):
    h = _rms(x_ref[...], g_ref[...]).astype(bf16)
    y = jnp.dot(h, w_ref[...], preferred_element_type=f32)
    o_ref[...] = y.reshape(o_ref.shape)


def _s5_proj(x3, g, w):
    bsz, seq, _ = x3.shape
    tok = min(S5_TOK, seq)
    nck = tok // S5_CHUNK
    return pl.pallas_call(
        _s5_proj_kernel,
        out_shape=jax.ShapeDtypeStruct((seq // S5_CHUNK, bsz, S5_CHUNK, MIX), f32),
        grid=(bsz, seq // tok),
        in_specs=[
            pl.BlockSpec((None, tok, D_MODEL), lambda b, i: (b, i, 0)),
            pl.BlockSpec((1, D_MODEL), lambda b, i: (0, 0)),
            pl.BlockSpec((D_MODEL, MIX), lambda b, i: (0, 0)),
        ],
        out_specs=pl.BlockSpec((nck, None, S5_CHUNK, MIX), lambda b, i: (i, b, 0, 0)),
        compiler_params=_cparams(2),
        name="s5_proj",
    )(x3, g, w)


def _s5_kernel(u_ref, k_ref, f_ref, e_ref, pw_ref, y_ref, t_ref, lhs_ref, vr_ref, vi_ref, xr_ref, xi_ref, *, mh, bsz):
    c = S5_CHUNK
    d = pl.program_id(1)

    @pl.when(pl.program_id(2) == 0)
    def _():
        for j in range(c):
            for i in range(c):
                t_ref[j * 128 : (j + 1) * 128, i * 128 : (i + 1) * 128] = k_ref[i - j + c - 1]
        xr_ref[...] = jnp.zeros_like(xr_ref)
        xi_ref[...] = jnp.zeros_like(xi_ref)

    for j in range(c):
        lhs_ref[:, j * 128 : (j + 1) * 128] = u_ref[pl.ds(j, mh, stride=c), :].astype(bf16)
    lhs = lhs_ref[...]
    y = jnp.dot(lhs, t_ref[...], preferred_element_type=f32)
    vr_ref[...] = jnp.dot(lhs, f_ref[:, :S5_ST], preferred_element_type=f32)
    vi_ref[...] = jnp.dot(lhs, f_ref[:, S5_ST:], preferred_element_type=f32)
    ar = pw_ref[0:1, :]
    ai = pw_ref[1:2, :]
    n_loc = mh // bsz

    def body(n, carry):
        xr, xi = carry
        ci = jnp.where(d == 0, n, n_loc - 1 - n)
        rows = pl.ds(pl.multiple_of(ci * bsz, bsz), bsz)
        vr = vr_ref[rows, :]
        vi = vi_ref[rows, :]
        vr_ref[rows, :] = xr
        vi_ref[rows, :] = xi
        return ar * xr - ai * xi + vr, ar * xi + ai * xr + vi

    xr, xi = lax.fori_loop(0, n_loc, body, (xr_ref[...], xi_ref[...]))
    xr_ref[...] = xr
    xi_ref[...] = xi
    y = y + jnp.dot(vr_ref[...].astype(bf16), e_ref[:S5_ST, :], preferred_element_type=f32)
    y = y + jnp.dot(vi_ref[...].astype(bf16), e_ref[S5_ST:, :], preferred_element_type=f32)
    for i in range(c):
        y_ref[pl.ds(i, mh, stride=c), :] = y[:, i * 128 : (i + 1) * 128]


def _s5_core(u4, kx, fm, em, pw):
    n_chunks, bsz, c, _ = u4.shape
    rows = n_chunks * bsz * c
    n_part = 2 if n_chunks % 2 == 0 else 1
    rp = rows // n_part
    mh = rp // c
    part = lambda d, h: h + d * (n_part - 1 - 2 * h)
    once = dict(pipeline_mode=pl.Buffered(1))
    return pl.pallas_call(
        functools.partial(_s5_kernel, mh=mh, bsz=bsz),
        out_shape=jax.ShapeDtypeStruct((2, rows, MIX), f32),
        grid=(S5_NSG, 2, n_part),
        in_specs=[
            pl.BlockSpec((rp, 128), lambda s, d, h: (part(d, h), s)),
            pl.BlockSpec((None, None, 2 * c - 1, 128, 128), lambda s, d, h: (d, s, 0, 0, 0), **once),
            pl.BlockSpec((None, None, S5_W, 2 * S5_ST), lambda s, d, h: (d, s, 0, 0), **once),
            pl.BlockSpec((None, None, 2 * S5_ST, S5_W), lambda s, d, h: (d, s, 0, 0), **once),
            pl.BlockSpec((None, None, 8, S5_ST), lambda s, d, h: (d, s, 0, 0)),
        ],
        out_specs=pl.BlockSpec((None, rp, 128), lambda s, d, h: (d, part(d, h), s)),
        scratch_shapes=[
            pltpu.VMEM((S5_W, S5_W), bf16),
            pltpu.VMEM((mh, S5_W), bf16),
            pltpu.VMEM((mh, S5_ST), f32),
            pltpu.VMEM((mh, S5_ST), f32),
            pltpu.VMEM((bsz, S5_ST), f32),
            pltpu.VMEM((bsz, S5_ST), f32),
        ],
        compiler_params=_cparams(3),
        name="s5_core",
    )(u4.reshape(rows, MIX), kx, fm, em, pw)


def _s5_post_kernel(yf_ref, yb_ref, u_ref, d_ref, w_ref, b_ref, o_ref):
    n = o_ref.shape[0]
    y = (yf_ref[...] + yb_ref[...] + d_ref[...] * u_ref[...]).reshape(n, MIX)
    y = 0.5 * y * (1.0 + jnp.tanh(math.sqrt(2.0 / math.pi) * (y + 0.044715 * (y * y * y))))
    o_ref[...] = (y * _sigmoid(_mm(y, w_ref[...]) + b_ref[...])).astype(o_ref.dtype)


def _s5_post(y2, u4, d_skip, glu_w, glu_b):
    n_chunks, bsz, c, _ = u4.shape
    seq = n_chunks * c
    tok = min(S5_TOK, seq)
    nck = tok // c
    y5 = y2.reshape(2, n_chunks, bsz, c, MIX)
    row = pl.BlockSpec((1, MIX), lambda b, i: (0, 0))
    return pl.pallas_call(
        _s5_post_kernel,
        out_shape=jax.ShapeDtypeStruct((bsz * seq, MIX), bf16),
        grid=(bsz, seq // tok),
        in_specs=[
            pl.BlockSpec((None, nck, None, c, MIX), lambda b, i: (0, i, b, 0, 0)),
            pl.BlockSpec((None, nck, None, c, MIX), lambda b, i: (1, i, b, 0, 0)),
            pl.BlockSpec((nck, None, c, MIX), lambda b, i: (i, b, 0, 0)),
            row,
            pl.BlockSpec((MIX, MIX), lambda b, i: (0, 0)),
            row,
        ],
        out_specs=pl.BlockSpec((tok, MIX), lambda b, i: (b * (seq // tok) + i, 0)),
        compiler_params=_cparams(2),
        name="s5_post",
    )(y5, y5, u4, d_skip, glu_w, glu_b)


def _s5_operators(a_re, a_im, log_dt, b_re, b_im, c_re, c_im):
    c = S5_CHUNK
    dt = jnp.exp(log_dt)[..., None]
    mag = a_re * dt
    ph = a_im * dt
    steps = jnp.arange(c + 1, dtype=f32)[:, None, None, None]
    m = jnp.exp(mag[None] * steps)
    pr = m * jnp.cos(ph[None] * steps)
    pi = m * jnp.sin(ph[None] * steps)
    den = a_re * a_re + a_im * a_im
    qr = ((pr[1] - 1.0) * a_re + pi[1] * a_im) / den
    qi = (pi[1] * a_re - (pr[1] - 1.0) * a_im) / den
    bbr = (qr[..., None] * b_re - qi[..., None] * b_im).transpose(0, 1, 3, 2)
    bbi = (qr[..., None] * b_im + qi[..., None] * b_re).transpose(0, 1, 3, 2)
    car = c_re[None, None] * pr[:, :, :, None, :] - c_im[None, None] * pi[:, :, :, None, :]
    cai = c_re[None, None] * pi[:, :, :, None, :] + c_im[None, None] * pr[:, :, :, None, :]
    kmat = jnp.einsum("tdgop,dgip->tdgio", car, bbr, precision=HI) - jnp.einsum("tdgop,dgip->tdgio", cai, bbi, precision=HI)
    kmat = kmat[:c]
    zeros = jnp.zeros((c - 1,) + kmat.shape[2:], f32)
    k_f = jnp.concatenate([zeros, kmat[:, 0]], axis=0)
    k_b = jnp.concatenate([kmat[::-1, 1], zeros], axis=0)
    eye = jnp.eye(S5_SG, dtype=f32)

    def tiles(a, lead):
        sh = a.shape
        a = a.reshape(sh[:lead] + (S5_NSG, S5_SG) + sh[lead + 1 :])
        a = a[..., :, :, None, :] * eye.reshape((1,) * (lead + 1) + (S5_SG, 1, S5_SG, 1))
        return a.reshape(sh[:lead] + (S5_NSG, S5_SG * sh[-2], S5_SG * sh[-1]))

    kx = tiles(jnp.stack([k_f, k_b], axis=0), 2).transpose(0, 2, 1, 3, 4)
    fsel = jnp.stack([c - 1 - jnp.arange(c), jnp.arange(c)], axis=0)
    d_idx = jnp.arange(2)[:, None]
    prj = pr[fsel, d_idx]
    pij = pi[fsel, d_idx]
    f_r = prj[:, :, :, None, :] * bbr[:, None] - pij[:, :, :, None, :] * bbi[:, None]
    f_i = prj[:, :, :, None, :] * bbi[:, None] + pij[:, :, :, None, :] * bbr[:, None]
    ft = lambda a: tiles(a, 2).transpose(0, 2, 1, 3, 4).reshape(2, S5_NSG, S5_W, S5_ST)
    fm = jnp.concatenate([ft(f_r), ft(f_i)], axis=-1)
    esel = jnp.stack([jnp.arange(c) + 1, c - jnp.arange(c)], axis=0)
    e_r = car[esel, d_idx].transpose(0, 1, 2, 4, 3)
    e_i = (-cai[esel, d_idx]).transpose(0, 1, 2, 4, 3)
    et = lambda a: tiles(a, 2).transpose(0, 2, 3, 1, 4).reshape(2, S5_NSG, S5_ST, S5_W)
    em = jnp.concatenate([et(e_r), et(e_i)], axis=-2)
    pw = jnp.stack([pr[c], pi[c]], axis=2).reshape(2, S5_NSG, S5_SG, 2, S5_P).transpose(0, 1, 3, 2, 4).reshape(2, S5_NSG, 2, S5_ST)
    pw = jnp.concatenate([pw, jnp.zeros((2, S5_NSG, 6, S5_ST), f32)], axis=2)
    return kx.astype(bf16), fm.astype(bf16), em.astype(bf16), pw


def _block_diag_ones(block, scale=1.0):
    idx = jnp.arange(MIX) // block
    return ((idx[:, None] == idx[None, :]).astype(f32) * scale).astype(bf16)


def kernel(x, p, ffn1_norm, ffn1_w1, ffn1_w3, ffn1_w2, mix_norm, w_in, rwkv_mu, rwkv_w0, rwkv_w2, rwkv_a0, rwkv_a2, rwkv_g2, rwkv_k_k, rwkv_k_a, rwkv_r_k, rwkv_ln_w, rwkv_ln_b, dn_conv, dn_a_log, dn_dt_bias, dn_norm, s5_a_re, s5_a_im, s5_log_dt, s5_b_re, s5_b_im, s5_c_re, s5_c_im, s5_d, s5_glu_w, s5_glu_b, w_branch_a, w_branch_b, w_branch_c, w_out, ffn2_norm, ffn2_w1, ffn2_w3, ffn2_w2, ple_norm, ple_w_gate, ple_w_proj, final_norm):
    bsz, seq, _ = x.shape
    t = bsz * seq
    depth = p.shape[0]
    row = lambda a: a.reshape(1, -1).astype(f32)
    ones64 = _block_diag_ones(RW_HD)
    avg64 = _block_diag_ones(RW_HD, 1.0 / RW_HD)
    ones128 = _block_diag_ones(DN_HD)
    avg128 = _block_diag_ones(DN_HD, 1.0 / DN_HD)
    o_rw, o_dn, o_s5, o_gate = 0, 3488, 3488 + 4128, 3488 + 4128 + 1024

    xt = x.reshape(t, D_MODEL)
    for i in range(depth):
        xt = _ffn(xt, row(ffn1_norm[i]), ffn1_w1[i].astype(bf16), ffn1_w3[i].astype(bf16), ffn1_w2[i].astype(bf16))

        wi = w_in[i]
        g_mix = row(mix_norm[i])
        w_all = jnp.concatenate([
            wi[:, o_rw : o_rw + 3488], jnp.zeros((D_MODEL, PROJ_DNS - 3488), f32),
            wi[:, o_dn + 4 * MIX : o_dn + 4128], jnp.zeros((D_MODEL, PROJ_DN - PROJ_DNS - 4 * DN_HEADS), f32),
            wi[:, o_dn : o_dn + 4 * MIX]], axis=1).astype(bf16)
        w_s5 = wi[:, o_s5 : o_s5 + MIX].astype(bf16)
        w_gate = wi[:, o_gate:].astype(bf16)
        c_all = _norm_mm(xt, g_mix, w_all).reshape(bsz, seq, PROJ_COLS)

        mu = jnp.concatenate([rwkv_mu[i], jnp.zeros((RW_COLS - 3488,), f32)]).reshape(1, RW_COLS)
        zeros_blk = jnp.zeros((64, MIX), f32)
        w2cat = jnp.concatenate([
            jnp.concatenate([rwkv_w2[i, 0], zeros_blk], axis=1),
            jnp.concatenate([zeros_blk, rwkv_w2[i, 1]], axis=1)], axis=0).astype(bf16)
        a2cat = jnp.concatenate([
            jnp.concatenate([rwkv_a2[i, 0], zeros_blk], axis=1),
            jnp.concatenate([zeros_blk, rwkv_a2[i, 1]], axis=1)], axis=0).astype(bf16)
        g2pad = jnp.concatenate([rwkv_g2[i], jnp.zeros((256 - 160, MIX), f32)], axis=0).astype(bf16)
        prep = _rwkv_prep(c_all, mu, row(rwkv_w0[i]), row(rwkv_a0[i]), w2cat, a2cat, g2pad,
                          row(rwkv_k_k[i]), row(rwkv_k_a[i]), row(rwkv_r_k[i]), ones64)
        r_, v_, an_, lwf_, lwb_, kf_, kb_, bf_, bb_, gate_, bonus_ = prep
        yf_rw, yb_rw = _rwkv_scan(r_, v_, an_, lwf_, lwb_, kf_, kb_, bf_, bb_)
        ya = _rwkv_post(yf_rw.reshape(t, MIX), yb_rw.reshape(t, MIX), bonus_.reshape(t, MIX), gate_.reshape(t, MIX),
                        row(rwkv_ln_w[i]), row(rwkv_ln_b[i]), avg64)

        conv8 = jnp.concatenate([dn_conv[i], jnp.zeros((8 - DN_CONV, 3 * MIX), f32)], axis=0)
        q_, k_, v2_ = _dn_prep(c_all, conv8, ones128)
        pad16 = lambda a: jnp.concatenate([jnp.zeros((2 * DN_HEADS,), f32), a.reshape(-1), jnp.zeros((128 - 4 * DN_HEADS,), f32)]).reshape(1, 128)
        of_dn, ob_dn = _dn_scan(q_, k_, v2_, c_all, pad16(dn_a_log[i]), pad16(dn_dt_bias[i]))
        yb = _dn_post(of_dn.reshape(t, MIX), ob_dn.reshape(t, MIX), c_all.reshape(t, PROJ_COLS), jnp.tile(dn_norm[i], DN_HEADS).reshape(1, MIX), avg128)

        u4 = _s5_proj(xt.reshape(bsz, seq, D_MODEL), g_mix, w_s5)
        ops5 = _s5_operators(s5_a_re[i], s5_a_im[i], s5_log_dt[i], s5_b_re[i], s5_b_im[i], s5_c_re[i], s5_c_im[i])
        yc = _s5_post(_s5_core(u4, *ops5), u4, row(s5_d[i]), s5_glu_w[i].astype(bf16), row(s5_glu_b[i]))

        xt = _merge(xt, g_mix, ya, yb, yc, w_gate, w_branch_a[i].astype(bf16), w_branch_b[i].astype(bf16),
                    w_branch_c[i].astype(bf16), w_out[i].astype(bf16))
        xt = _ffn(xt, row(ffn2_norm[i]), ffn2_w1[i].astype(bf16), ffn2_w3[i].astype(bf16), ffn2_w2[i].astype(bf16))
        xt = _ple(xt, p[i].reshape(t, PLE_DIM), row(ple_norm[i]), ple_w_gate[i].astype(bf16), ple_w_proj[i].astype(bf16))
    return _final_norm(xt, row(final_norm)).reshape(bsz, seq, D_MODEL)
```

```python
import functools
import math

import jax
import jax.numpy as jnp
from jax import lax
from jax.experimental import pallas as pl
from jax.experimental.pallas import tpu as pltpu

f32 = jnp.float32
bf16 = jnp.bfloat16
HI = lax.Precision.HIGHEST

D_MODEL = 2048
D_FF = 5632
PLE_DIM = 256
NORM_EPS = 1e-6
MIX = 1024

RW_HEADS = 16
RW_HD = 64
RW_DECAY_SCALE = math.exp(-0.5)
RW_GN_EPS = 64e-5
RW_COLS = 3584
RW_CHUNK = 64

DN_HEADS = 8
DN_HD = 128
DN_CONV = 5
DN_CHUNK = 64

S5_G = 64
S5_CG = 16
S5_P = 64
S5_CHUNK = 16

PROJ_RW = 0
PROJ_DNS = 3584
PROJ_DN = 4096
PROJ_COLS = 8192

VMEM_LIMIT = 56 * 1024 * 1024


def _cparams(n_axes):
    return pltpu.CompilerParams(
        dimension_semantics=("arbitrary",) * n_axes, vmem_limit_bytes=VMEM_LIMIT
    )


def _mm(a, b):
    return jnp.dot(a.astype(bf16), b.astype(bf16), preferred_element_type=f32)


def _bmm(a, b):
    return lax.dot_general(a.astype(bf16), b.astype(bf16), (((2,), (1,)), ((0,), (0,))), preferred_element_type=f32)


def _bmm_nt(a, b):
    return lax.dot_general(a.astype(bf16), b.astype(bf16), (((2,), (2,)), ((0,), (0,))), preferred_element_type=f32)


def _bmm_tn(a, b):
    return lax.dot_general(a.astype(bf16), b.astype(bf16), (((1,), (1,)), ((0,), (0,))), preferred_element_type=f32)


def _mm2(a, b01):
    hi = a.astype(bf16)
    lo = (a - hi.astype(f32)).astype(bf16)
    return jnp.dot(hi, b01, preferred_element_type=f32) + jnp.dot(lo, b01, preferred_element_type=f32)


def _mm3(m01, x):
    hi = x.astype(bf16)
    r1 = x - hi.astype(f32)
    mid = r1.astype(bf16)
    lo = (r1 - mid.astype(f32)).astype(bf16)
    dot = lambda p: jnp.dot(m01, p, preferred_element_type=f32)
    return dot(hi) + dot(mid) + dot(lo)


def _rms(x, g):
    return x * lax.rsqrt(jnp.mean(x * x, axis=-1, keepdims=True) + NORM_EPS) * g


def _sigmoid(x):
    return 1.0 / (1.0 + jnp.exp(-x))


def _silu(x):
    return x * _sigmoid(x)


def _softplus(x):
    return jnp.maximum(x, 0.0) + jnp.log(1.0 + jnp.exp(-jnp.abs(x)))


def _ffn_kernel(x_ref, g_ref, w1_ref, w3_ref, w2_ref, o_ref, h_ref):
    @pl.when(pl.program_id(1) == 0)
    def _():
        x = x_ref[...]
        h_ref[...] = _rms(x, g_ref[...]).astype(bf16)
        o_ref[...] = x

    h = h_ref[...]
    a = jnp.dot(h, w1_ref[...], preferred_element_type=f32)
    b = jnp.dot(h, w3_ref[...], preferred_element_type=f32)
    act = (0.5 * _silu(a) * b).astype(bf16)
    o_ref[...] += jnp.dot(act, w2_ref[...], preferred_element_type=f32)


def _ffn(x, g, w1, w3, w2, tm=512, tf=512):
    t = x.shape[0]
    tm = min(tm, t)
    return pl.pallas_call(
        _ffn_kernel,
        out_shape=jax.ShapeDtypeStruct((t, D_MODEL), f32),
        grid=(t // tm, D_FF // tf),
        in_specs=[
            pl.BlockSpec((tm, D_MODEL), lambda i, j: (i, 0)),
            pl.BlockSpec((1, D_MODEL), lambda i, j: (0, 0)),
            pl.BlockSpec((D_MODEL, tf), lambda i, j: (0, j)),
            pl.BlockSpec((D_MODEL, tf), lambda i, j: (0, j)),
            pl.BlockSpec((tf, D_MODEL), lambda i, j: (j, 0)),
        ],
        out_specs=pl.BlockSpec((tm, D_MODEL), lambda i, j: (i, 0)),
        scratch_shapes=[pltpu.VMEM((tm, D_MODEL), bf16)],
        compiler_params=_cparams(2),
        name="ffn",
    )(x, g, w1, w3, w2)


def _norm_mm_kernel(x_ref, g_ref, w_ref, o_ref, h_ref):
    @pl.when(pl.program_id(1) == 0)
    def _():
        h_ref[...] = _rms(x_ref[...], g_ref[...]).astype(bf16)

    o_ref[...] = jnp.dot(h_ref[...], w_ref[...], preferred_element_type=f32)


def _norm_mm(x, g, w, tm=1024, tn=512):
    t = x.shape[0]
    n = w.shape[1]
    tm = min(tm, t)
    return pl.pallas_call(
        _norm_mm_kernel,
        out_shape=jax.ShapeDtypeStruct((t, n), f32),
        grid=(t // tm, n // tn),
        in_specs=[
            pl.BlockSpec((tm, D_MODEL), lambda i, j: (i, 0)),
            pl.BlockSpec((1, D_MODEL), lambda i, j: (0, 0)),
            pl.BlockSpec((D_MODEL, tn), lambda i, j: (0, j)),
        ],
        out_specs=pl.BlockSpec((tm, tn), lambda i, j: (i, j)),
        scratch_shapes=[pltpu.VMEM((tm, D_MODEL), bf16)],
        compiler_params=_cparams(2),
        name="norm_mm",
    )(x, g, w)


def _merge_kernel(x_ref, g_ref, ya_ref, yb_ref, yc_ref, wga_ref, wgb_ref, wgc_ref, wa_ref, wb_ref, wc_ref, wo_ref, o_ref, h_ref):
    @pl.when(pl.program_id(1) == 0)
    def _():
        x = x_ref[...]
        h_ref[...] = _rms(x, g_ref[...]).astype(bf16)
        o_ref[...] = x

    h = h_ref[...]
    m = None
    for y_ref, wg_ref, w_ref in ((ya_ref, wga_ref, wa_ref), (yb_ref, wgb_ref, wb_ref), (yc_ref, wgc_ref, wc_ref)):
        gate = _sigmoid(jnp.dot(h, wg_ref[...], preferred_element_type=f32))
        term = gate * jnp.dot(y_ref[...], w_ref[...], preferred_element_type=f32)
        m = term if m is None else m + term
    o_ref[...] += jnp.dot(m.astype(bf16), wo_ref[...], preferred_element_type=f32)


def _merge(x, g, ya, yb, yc, w_gate, wa, wb, wc, wo, tm=512, tn=512):
    t = x.shape[0]
    tm = min(tm, t)
    nj = D_MODEL // tn
    tok = pl.BlockSpec((tm, MIX), lambda i, j: (i, 0))
    gate_w = lambda b: pl.BlockSpec((D_MODEL, tn), lambda i, j: (0, b * nj + j))
    br_w = pl.BlockSpec((MIX, tn), lambda i, j: (0, j))
    return pl.pallas_call(
        _merge_kernel,
        out_shape=jax.ShapeDtypeStruct((t, D_MODEL), f32),
        grid=(t // tm, nj),
        in_specs=[
            pl.BlockSpec((tm, D_MODEL), lambda i, j: (i, 0)),
            pl.BlockSpec((1, D_MODEL), lambda i, j: (0, 0)),
            tok, tok, tok,
            gate_w(0), gate_w(1), gate_w(2),
            br_w, br_w, br_w,
            pl.BlockSpec((tn, D_MODEL), lambda i, j: (j, 0)),
        ],
        out_specs=pl.BlockSpec((tm, D_MODEL), lambda i, j: (i, 0)),
        scratch_shapes=[pltpu.VMEM((tm, D_MODEL), bf16)],
        compiler_params=_cparams(2),
        name="merge",
    )(x, g, ya, yb, yc, w_gate, w_gate, w_gate, wa, wb, wc, wo)


def _ple_kernel(x_ref, xj_ref, p_ref, g_ref, wg_ref, wp_ref, o_ref, h_ref):
    @pl.when(pl.program_id(1) == 0)
    def _():
        h_ref[...] = _rms(x_ref[...], g_ref[...]).astype(bf16)

    gate = _sigmoid(jnp.dot(h_ref[...], wg_ref[...], preferred_element_type=f32))
    emb = jnp.dot(p_ref[...].astype(bf16), wp_ref[...], preferred_element_type=f32)
    o_ref[...] = xj_ref[...] + emb * gate


def _ple(x, p, g, wg, wp, tm=512, tn=512):
    t = x.shape[0]
    tm = min(tm, t)
    return pl.pallas_call(
        _ple_kernel,
        out_shape=jax.ShapeDtypeStruct((t, D_MODEL), f32),
        grid=(t // tm, D_MODEL // tn),
        in_specs=[
            pl.BlockSpec((tm, D_MODEL), lambda i, j: (i, 0)),
            pl.BlockSpec((tm, tn), lambda i, j: (i, j)),
            pl.BlockSpec((tm, PLE_DIM), lambda i, j: (i, 0)),
            pl.BlockSpec((1, D_MODEL), lambda i, j: (0, 0)),
            pl.BlockSpec((D_MODEL, tn), lambda i, j: (0, j)),
            pl.BlockSpec((PLE_DIM, tn), lambda i, j: (0, j)),
        ],
        out_specs=pl.BlockSpec((tm, tn), lambda i, j: (i, j)),
        scratch_shapes=[pltpu.VMEM((tm, D_MODEL), bf16)],
        compiler_params=_cparams(2),
        name="ple",
    )(x, x, p, g, wg, wp)


def _final_norm_kernel(x_ref, g_ref, o_ref):
    o_ref[...] = _rms(x_ref[...], g_ref[...])


def _final_norm(x, g, tm=512):
    t = x.shape[0]
    tm = min(tm, t)
    return pl.pallas_call(
        _final_norm_kernel,
        out_shape=jax.ShapeDtypeStruct((t, D_MODEL), f32),
        grid=(t // tm,),
        in_specs=[
            pl.BlockSpec((tm, D_MODEL), lambda i: (i, 0)),
            pl.BlockSpec((1, D_MODEL), lambda i: (0, 0)),
        ],
        out_specs=pl.BlockSpec((tm, D_MODEL), lambda i: (i, 0)),
        compiler_params=_cparams(1),
        name="final_norm",
    )(x, g)


def _halo(prev_ref, c_ref, next_ref, n_tiles):
    i = pl.program_id(1)
    prev = jnp.where(i == 0, 0.0, prev_ref[...])
    nxt = jnp.where(i == n_tiles - 1, 0.0, next_ref[...])
    return jnp.concatenate([prev, c_ref[...], nxt], axis=0)


def _shifted(ext, k, tl):
    n = ext.shape[0]
    return pltpu.roll(ext, (-k) % n, axis=0)[8 : 8 + tl]


def _iota2(c):
    return lax.broadcasted_iota(jnp.int32, (c, c), 0), lax.broadcasted_iota(jnp.int32, (c, c), 1)


def _by_dir(x, fn_f, fn_b):
    n = x.shape[0] // 2
    return jnp.concatenate([fn_f(x[:n]), fn_b(x[n:])], axis=0)


def _mask_dir(x, mask_f, mask_b):
    return _by_dir(x, lambda a: jnp.where(mask_f, a, 0.0), lambda a: jnp.where(mask_b, a, 0.0))


def _unit_tri_inverse(m, c):
    ii, jj = _iota2(c)
    same = lambda sh: (ii >> sh) == (jj >> sh)
    d = (ii == jj).astype(f32) + _mask_dir(m, same(1) & (ii > jj), same(1) & (jj > ii))
    sh = 1
    while (1 << sh) < c:
        hi_i = ((ii >> sh) & 1) == 1
        hi_j = ((jj >> sh) & 1) == 1
        ms = _mask_dir(m, same(sh + 1) & hi_i & ~hi_j, same(sh + 1) & hi_j & ~hi_i)
        d = d + _bmm(d, _bmm(ms, d))
        sh += 1
    return d


def _rwkv_prep_kernel(
    prev_ref, c_ref, next_ref, mu_ref, w0_ref, a0_ref, w2_ref, a2_ref, g2_ref, kk_ref, ka_ref, rk_ref, ones_ref,
    r_o, v_o, an_o, lwf_o, lwb_o, kf_o, kb_o, bf_o, bb_o, gate_o, bonus_o, *, tl, n_tiles
):
    ext = _halo(prev_ref, c_ref, next_ref, n_tiles)
    c = ext[8 : 8 + tl]
    cs = c + (0.5 * (_shifted(ext, -1, tl) + _shifted(ext, 1, tl)) - c) * mu_ref[...]
    r = cs[:, 0:MIX]
    k = cs[:, MIX : 2 * MIX]
    v = cs[:, 2 * MIX : 3 * MIX]
    lw_in = jnp.tanh(cs[:, 3 * MIX : 3 * MIX + 128])
    la_in = cs[:, 3 * MIX + 128 : 3 * MIX + 256]
    lg_in = _sigmoid(cs[:, 3 * MIX + 256 : 3 * MIX + 512])
    logw = -RW_DECAY_SCALE * _sigmoid(w0_ref[...] + _mm(lw_in, w2_ref[...]))
    a = _sigmoid(a0_ref[...] + _mm(la_in, a2_ref[...]))
    gate_o[...] = _mm(lg_in, g2_ref[...]).astype(gate_o.dtype)
    ones = ones_ref[...]
    kx = k * kk_ref[...]
    kk = kx * lax.rsqrt(_mm2(kx * kx, ones) + 1e-6)
    a_f = a[:, 0:MIX]
    a_b = a[:, MIX : 2 * MIX]
    ka = ka_ref[...]
    k_f = k * (1.0 + (a_f - 1.0) * ka)
    k_b = k * (1.0 + (a_b - 1.0) * ka)
    r_o[...] = r.astype(r_o.dtype)
    v_o[...] = v.astype(v_o.dtype)
    an_o[...] = (-kk).astype(an_o.dtype)
    lwf_o[...] = logw[:, 0:MIX]
    lwb_o[...] = logw[:, MIX : 2 * MIX]
    kf_o[...] = k_f.astype(kf_o.dtype)
    kb_o[...] = k_b.astype(kb_o.dtype)
    bf_o[...] = (kk * a_f).astype(bf_o.dtype)
    bb_o[...] = (kk * a_b).astype(bb_o.dtype)
    bonus_o[...] = (_mm2(r * (k_f + k_b) * rk_ref[...], ones) * v).astype(bonus_o.dtype)


def _rwkv_prep(c, mu, w0, a0, w2, a2, g2, k_k, k_a, r_k, ones64, tl=128):
    bsz, seq, _ = c.shape
    tl = min(tl, seq)
    n_tiles = seq // tl
    nb = tl // 8
    row = lambda a: pl.BlockSpec(a.shape, lambda b, i: (0,) * a.ndim)
    outs = [jax.ShapeDtypeStruct((bsz, seq, MIX), f32 if n in (3, 4) else bf16) for n in range(11)]
    ospec = pl.BlockSpec((None, tl, MIX), lambda b, i: (b, i, 0))
    return pl.pallas_call(
        functools.partial(_rwkv_prep_kernel, tl=tl, n_tiles=n_tiles),
        out_shape=outs,
        grid=(bsz, n_tiles),
        in_specs=[
            pl.BlockSpec((None, 8, RW_COLS), lambda b, i: (b, jnp.maximum(i * nb - 1, 0), 0)),
            pl.BlockSpec((None, tl, RW_COLS), lambda b, i: (b, i, 0)),
            pl.BlockSpec((None, 8, RW_COLS), lambda b, i: (b, jnp.minimum((i + 1) * nb, seq // 8 - 1), 0)),
            row(mu), row(w0), row(a0), row(w2), row(a2), row(g2), row(k_k), row(k_a), row(r_k), row(ones64),
        ],
        out_specs=[ospec] * 11,
        compiler_params=_cparams(2),
        name="rwkv_prep",
    )(c, c, c, mu, w0, a0, w2, a2, g2, k_k, k_a, r_k, ones64)


def _rwkv_chunks(s, at, rt, bt, kt, v, bh, kh, e_tot):
    c = RW_CHUNK
    ii, jj = _iota2(c)
    ar = jnp.concatenate([at, rt], axis=1)
    rb = _bmm_nt(ar, bt)
    rk = _bmm_nt(ar, kt)
    a_ab = _mask_dir(rb[:, :c], jj < ii, jj > ii)
    a_ak = _mask_dir(rk[:, :c], jj < ii, jj > ii)
    p_b = _mask_dir(rb[:, c:], jj <= ii, jj >= ii)
    p_k = _mask_dir(rk[:, c:], jj <= ii, jj >= ii)
    tinv = _unit_tri_inverse(a_ab, c)
    w = _bmm(tinv, at)
    u0 = _bmm(tinv, _bmm(a_ak, v))
    ws = _bmm_nt(jnp.concatenate([w.astype(bf16), rt], axis=1), s)
    u = ws[:, :c] + u0
    y = ws[:, c:] + _bmm(p_b, u) + _bmm(p_k, v)
    s_new = s * e_tot + _bmm_tn(jnp.concatenate([u.astype(bf16), v], axis=1), jnp.concatenate([bh, kh], axis=1))
    return y, s_new


def _rwkv_scan_kernel(rf, vf, anf, lwf, kf, bf_, rb, vb, anb, lwb, kb, bb, yf_ref, yb_ref, s_ref):
    c = RW_CHUNK

    @pl.when(pl.program_id(1) == 0)
    def _():
        s_ref[...] = jnp.zeros_like(s_ref)

    ii, jj = _iota2(c)

    def side(r_ref, v_ref, an_ref, lw_ref, k_ref, b_ref, cum, last):
        lw = lw_ref[...]
        cl = _mm3(cum.astype(bf16), lw)
        e_neg = jnp.exp(-cl)
        cl_last = cl[last : last + 1, :]
        e_last = jnp.exp(cl_last - cl)
        b = b_ref[...].astype(f32)
        kd = k_ref[...].astype(f32)
        parts = (an_ref[...].astype(f32) * jnp.exp(cl - lw), r_ref[...].astype(f32) * jnp.exp(cl), b * e_neg, kd * e_neg, v_ref[...], b * e_last, kd * e_last)
        return [p.astype(bf16) for p in parts], jnp.exp(cl_last)

    pf, ef = side(rf, vf, anf, lwf, kf, bf_, jj <= ii, c - 1)
    pb, eb = side(rb, vb, anb, lwb, kb, bb, jj >= ii, 0)
    heads = lambda x: [x[:, h * RW_HD : (h + 1) * RW_HD] for h in range(RW_HEADS)]
    ops = [jnp.stack(heads(a) + heads(b), axis=0) for a, b in zip(pf, pb)]
    e_tot = jnp.stack(heads(ef) + heads(eb), axis=0)
    y, s_new = _rwkv_chunks(s_ref[...], *ops, e_tot)
    s_ref[...] = s_new
    for h in range(RW_HEADS):
        yf_ref[:, h * RW_HD : (h + 1) * RW_HD] = y[h].astype(yf_ref.dtype)
        yb_ref[:, h * RW_HD : (h + 1) * RW_HD] = y[RW_HEADS + h].astype(yb_ref.dtype)


def _rwkv_scan(r, v, an, lwf, lwb, kf, kb, bf_, bb):
    bsz, seq, _ = r.shape
    n_chunks = seq // RW_CHUNK
    fwd = pl.BlockSpec((None, RW_CHUNK, MIX), lambda b, i: (b, i, 0))
    bwd = pl.BlockSpec((None, RW_CHUNK, MIX), lambda b, i: (b, n_chunks - 1 - i, 0))
    out = jax.ShapeDtypeStruct((bsz, seq, MIX), bf16)
    return pl.pallas_call(
        _rwkv_scan_kernel,
        out_shape=[out, out],
        grid=(bsz, n_chunks),
        in_specs=[fwd] * 6 + [bwd] * 6,
        out_specs=[fwd, bwd],
        scratch_shapes=[pltpu.VMEM((2 * RW_HEADS, RW_HD, RW_HD), f32)],
        compiler_params=_cparams(2),
        name="rwkv_scan",
    )(r, v, an, lwf, kf, bf_, r, v, an, lwb, kb, bb)


def _rwkv_post_kernel(yf_ref, yb_ref, bonus_ref, gate_ref, lnw_ref, lnb_ref, avg_ref, o_ref):
    y = yf_ref[...].astype(f32) + yb_ref[...].astype(f32)
    avg = avg_ref[...]
    mean = _mm2(y, avg)
    yc = y - mean
    var = _mm2(yc * yc, avg)
    yn = yc * lax.rsqrt(var + RW_GN_EPS) * lnw_ref[...] + lnb_ref[...]
    o_ref[...] = ((yn + bonus_ref[...].astype(f32)) * gate_ref[...].astype(f32)).astype(o_ref.dtype)


def _rwkv_post(yf, yb, bonus, gate, ln_w, ln_b, avg64, tm=512):
    t = yf.shape[0]
    tm = min(tm, t)
    tok = pl.BlockSpec((tm, MIX), lambda i: (i, 0))
    row = pl.BlockSpec((1, MIX), lambda i: (0, 0))
    return pl.pallas_call(
        _rwkv_post_kernel,
        out_shape=jax.ShapeDtypeStruct((t, MIX), bf16),
        grid=(t // tm,),
        in_specs=[tok, tok, tok, tok, row, row, pl.BlockSpec((MIX, MIX), lambda i: (0, 0))],
        out_specs=tok,
        compiler_params=_cparams(1),
        name="rwkv_post",
    )(yf, yb, bonus, gate, ln_w, ln_b, avg64)


def _dn_prep_kernel(*refs, tl, n_tiles):
    halos, (cw_ref, ones_ref, q_o, k_o, v_o) = refs[:9], refs[9:]
    cw = cw_ref[...]
    parts = []
    for p in range(3):
        ext = _halo(*halos[3 * p : 3 * p + 3], n_tiles)
        acc = None
        for j in range(DN_CONV):
            term = _shifted(ext, j - DN_CONV // 2, tl) * cw[j : j + 1, p * MIX : (p + 1) * MIX]
            acc = term if acc is None else acc + term
        parts.append(_silu(acc))
    q, k, v = parts
    ones = ones_ref[...]
    q_o[...] = (q * lax.rsqrt(_mm2(q * q, ones) + 1e-6) * (DN_HD ** -0.5)).astype(q_o.dtype)
    k_o[...] = (k * lax.rsqrt(_mm2(k * k, ones) + 1e-6)).astype(k_o.dtype)
    v_o[...] = v.astype(v_o.dtype)


def _dn_prep(c, conv_w8, ones128, tl=128):
    bsz, seq, _ = c.shape
    tl = min(tl, seq)
    n_tiles = seq // tl
    nb = tl // 8
    out = jax.ShapeDtypeStruct((bsz, seq, MIX), bf16)
    ospec = pl.BlockSpec((None, tl, MIX), lambda b, i: (b, i, 0))
    halo_specs = []
    for p in range(3):
        col = PROJ_DN // MIX + p
        halo_specs += [
            pl.BlockSpec((None, 8, MIX), lambda b, i, col=col: (b, jnp.maximum(i * nb - 1, 0), col)),
            pl.BlockSpec((None, tl, MIX), lambda b, i, col=col: (b, i, col)),
            pl.BlockSpec((None, 8, MIX), lambda b, i, col=col: (b, jnp.minimum((i + 1) * nb, seq // 8 - 1), col)),
        ]
    return pl.pallas_call(
        functools.partial(_dn_prep_kernel, tl=tl, n_tiles=n_tiles),
        out_shape=[out] * 3,
        grid=(bsz, n_tiles),
        in_specs=halo_specs + [
            pl.BlockSpec((8, 3 * MIX), lambda b, i: (0, 0)),
            pl.BlockSpec((MIX, MIX), lambda b, i: (0, 0)),
        ],
        out_specs=[ospec] * 3,
        compiler_params=_cparams(2),
        name="dn_prep",
    )(*([c] * 9), conv_w8, ones128)


def _dn_chunks(s, q, k, v, beta, gcol, grow):
    c = DN_CHUNK
    ii, jj = _iota2(c)
    incl_f, incl_b = jj <= ii, jj >= ii
    diff = gcol[:, :, :c] - grow
    decay = _by_dir(
        diff,
        lambda a: jnp.where(incl_f, jnp.exp(jnp.where(incl_f, a, 0.0)), 0.0),
        lambda a: jnp.where(incl_b, jnp.exp(jnp.where(incl_b, a, 0.0)), 0.0),
    )
    kb = k * beta
    r = _bmm_nt(jnp.concatenate([kb, q], axis=1), k)
    nmat = _mask_dir(r[:, :c] * decay, jj < ii, jj > ii)
    qk = r[:, c:] * decay
    tinv = _unit_tri_inverse(-nmat, c)
    e_gc = jnp.exp(gcol)
    uw = _bmm(tinv, jnp.concatenate([v * beta, kb * e_gc], axis=2))
    u = uw[:, :, :DN_HD]
    w = uw[:, :, DN_HD:]
    wq = _bmm(jnp.concatenate([w, q * e_gc], axis=1), s)
    v_new = u - wq[:, :c]
    o = wq[:, c:] + _bmm(qk, v_new)
    gl = _by_dir(gcol, lambda a: a[:, c - 1 : c, :], lambda a: a[:, 0:1, :])
    s_new = s * jnp.exp(gl) + _bmm_tn(k * jnp.exp(gl - gcol), v_new)
    return o, s_new


def _dn_scan_kernel(qf, kf, vf, gf, qb, kb, vb, gb, alog_ref, dtb_ref, of_ref, ob_ref, s_ref):
    c = DN_CHUNK

    @pl.when(pl.program_id(1) == 0)
    def _():
        s_ref[...] = jnp.zeros_like(s_ref)

    ii, jj = _iota2(c)
    qs, ks, vs, betas, gcols, grows = [], [], [], [], [], []
    for d, (q_ref, k_ref, v_ref, g_ref, cum) in enumerate(((qf, kf, vf, gf, jj <= ii), (qb, kb, vb, gb, jj >= ii))):
        x = g_ref[...]
        beta = _sigmoid(x)
        g = -jnp.exp(alog_ref[...]) * _softplus(x + dtb_ref[...])
        gc = _mm3(cum.astype(bf16), g)
        gct = gc.T
        q, k, v = q_ref[...].astype(f32), k_ref[...].astype(f32), v_ref[...].astype(f32)
        for h in range(DN_HEADS):
            col = 2 * DN_HEADS + DN_HEADS * d + h
            bcol = DN_HEADS * d + h
            sl = slice(h * DN_HD, (h + 1) * DN_HD)
            qs.append(q[:, sl])
            ks.append(k[:, sl])
            vs.append(v[:, sl])
            betas.append(jnp.broadcast_to(beta[:, bcol : bcol + 1], (c, DN_HD)))
            gcols.append(jnp.broadcast_to(gc[:, col : col + 1], (c, DN_HD)))
            grows.append(gct[col : col + 1, :])
    st = lambda xs: jnp.stack(xs, axis=0)
    o, s_new = _dn_chunks(s_ref[...], st(qs), st(ks), st(vs), st(betas), st(gcols), st(grows))
    s_ref[...] = s_new
    for h in range(DN_HEADS):
        of_ref[:, h * DN_HD : (h + 1) * DN_HD] = o[h].astype(of_ref.dtype)
        ob_ref[:, h * DN_HD : (h + 1) * DN_HD] = o[DN_HEADS + h].astype(ob_ref.dtype)


def _dn_scan(q, k, v, gates, alog_row, dtb_row):
    bsz, seq, _ = q.shape
    n_chunks = seq // DN_CHUNK
    fwd = lambda w, col=0: pl.BlockSpec((None, DN_CHUNK, w), lambda b, i: (b, i, col))
    bwd = lambda w, col=0: pl.BlockSpec((None, DN_CHUNK, w), lambda b, i: (b, n_chunks - 1 - i, col))
    gcol = PROJ_DNS // 128
    row = pl.BlockSpec((1, 128), lambda b, i: (0, 0))
    out = jax.ShapeDtypeStruct((bsz, seq, MIX), bf16)
    return pl.pallas_call(
        _dn_scan_kernel,
        out_shape=[out, out],
        grid=(bsz, n_chunks),
        in_specs=[fwd(MIX)] * 3 + [fwd(128, gcol)] + [bwd(MIX)] * 3 + [bwd(128, gcol)] + [row, row],
        out_specs=[fwd(MIX), bwd(MIX)],
        scratch_shapes=[pltpu.VMEM((2 * DN_HEADS, DN_HD, DN_HD), f32)],
        compiler_params=_cparams(2),
        name="dn_scan",
    )(q, k, v, gates, q, k, v, gates, alog_row, dtb_row)


def _dn_post_kernel(of_ref, ob_ref, z_ref, nw_ref, avg_ref, o_ref):
    o = of_ref[...].astype(f32) + ob_ref[...].astype(f32)
    o = o * lax.rsqrt(_mm2(o * o, avg_ref[...]) + NORM_EPS) * nw_ref[...]
    o_ref[...] = (o * _silu(z_ref[...])).astype(o_ref.dtype)


def _dn_post(of, ob, c_dn, norm_w, avg128, tm=512):
    t = of.shape[0]
    tm = min(tm, t)
    tok = pl.BlockSpec((tm, MIX), lambda i: (i, 0))
    return pl.pallas_call(
        _dn_post_kernel,
        out_shape=jax.ShapeDtypeStruct((t, MIX), bf16),
        grid=(t // tm,),
        in_specs=[tok, tok, pl.BlockSpec((tm, MIX), lambda i: (i, PROJ_DN // MIX + 3)),
                  pl.BlockSpec((1, MIX), lambda i: (0, 0)), pl.BlockSpec((MIX, MIX), lambda i: (0, 0))],
        out_specs=tok,
        compiler_params=_cparams(1),
        name="dn_post",
    )(of, ob, c_dn, norm_w, avg128)


S5_SG = 8
S5_NSG = S5_G // S5_SG
S5_ST = S5_SG * S5_P
S5_W = S5_CHUNK * 128
S5_TOK = 512


def _s5_proj_kernel(x_ref, g_ref, w_ref, o_ref):
    h = _rms(x_ref[...], g_ref[...]).astype(bf16)
    y = jnp.dot(h, w_ref[...], preferred_element_type=f32)
    o_ref[...] = y.reshape(o_ref.shape)


def _s5_proj(x3, g, w):
    bsz, seq, _ = x3.shape
    tok = min(S5_TOK, seq)
    nck = tok // S5_CHUNK
    return pl.pallas_call(
        _s5_proj_kernel,
        out_shape=jax.ShapeDtypeStruct((seq // S5_CHUNK, bsz, S5_CHUNK, MIX), f32),
        grid=(bsz, seq // tok),
        in_specs=[
            pl.BlockSpec((None, tok, D_MODEL), lambda b, i: (b, i, 0)),
            pl.BlockSpec((1, D_MODEL), lambda b, i: (0, 0)),
            pl.BlockSpec((D_MODEL, MIX), lambda b, i: (0, 0)),
        ],
        out_specs=pl.BlockSpec((nck, None, S5_CHUNK, MIX), lambda b, i: (i, b, 0, 0)),
        compiler_params=_cparams(2),
        name="s5_proj",
    )(x3, g, w)


def _mm3r(a, b01):
    hi = a.astype(bf16)
    r1 = a - hi.astype(f32)
    mid = r1.astype(bf16)
    lo = (r1 - mid.astype(f32)).astype(bf16)
    dot = lambda p: jnp.dot(p, b01, preferred_element_type=f32)
    return dot(hi) + dot(mid) + dot(lo)


def _s5_build_operators(d, bb_ref, cc_ref, prow_ref, pcol_ref, t_ref, f_ref, e_ref):
    c = S5_CHUNK
    io = lambda shape, ax: lax.broadcasted_iota(jnp.int32, shape, ax)
    tile_f = (io((S5_P, S5_ST), 0) == (io((S5_P, S5_ST), 1) & (S5_P - 1))).astype(bf16)
    mask_f = (io((128, S5_ST), 0) >> 4) == (io((128, S5_ST), 1) >> 6)
    bbr = jnp.where(mask_f, _mm3r(bb_ref[0], tile_f), 0.0)
    bbi = jnp.where(mask_f, _mm3r(bb_ref[1], tile_f), 0.0)
    tile_e = (io((S5_CG, 128), 0) == (io((S5_CG, 128), 1) & (S5_CG - 1))).astype(bf16)
    mask_e = (io((S5_ST, 128), 0) >> 6) == (io((S5_ST, 128), 1) >> 4)
    cre = jnp.where(mask_e, _mm3r(cc_ref[0], tile_e), 0.0)
    cim = jnp.where(mask_e, _mm3r(cc_ref[1], tile_e), 0.0)
    for j in range(c):
        pr = prow_ref[0, j : j + 1, :]
        pi = prow_ref[1, j : j + 1, :]
        f_ref[j * 128 : (j + 1) * 128, :S5_ST] = (pr * bbr - pi * bbi).astype(bf16)
        f_ref[j * 128 : (j + 1) * 128, S5_ST:] = (pr * bbi + pi * bbr).astype(bf16)
    pcr = pcol_ref[0]
    pci = pcol_ref[1]

    def c_times_power(col):
        pr = jnp.broadcast_to(pcr[:, col : col + 1], (S5_ST, 128))
        pi = jnp.broadcast_to(pci[:, col : col + 1], (S5_ST, 128))
        return cre * pr - cim * pi, -(cre * pi + cim * pr)

    for i in range(c):
        er, ei = c_times_power(i)
        e_ref[:S5_ST, i * 128 : (i + 1) * 128] = er.astype(bf16)
        e_ref[S5_ST:, i * 128 : (i + 1) * 128] = ei.astype(bf16)
    bb = jnp.concatenate([bbr, bbi], axis=1)
    ks = []
    for lag in range(c):
        er, ei = c_times_power(c + lag)
        ks.append(jnp.dot(bb, jnp.concatenate([er, ei], axis=0), precision=HI, preferred_element_type=f32).astype(bf16))
    t_ref[...] = jnp.zeros_like(t_ref)

    @pl.when(d == 0)
    def _():
        for lag in range(c):
            for j in range(c - lag):
                t_ref[j * 128 : (j + 1) * 128, (j + lag) * 128 : (j + lag + 1) * 128] = ks[lag]

    @pl.when(d == 1)
    def _():
        for lag in range(c):
            for j in range(c - lag):
                t_ref[(j + lag) * 128 : (j + lag + 1) * 128, j * 128 : (j + 1) * 128] = ks[lag]


def _s5_kernel(u_ref, bb_ref, cc_ref, prow_ref, pcol_ref, y_ref, t_ref, f_ref, e_ref, lhs_ref, vr_ref, vi_ref, xr_ref, xi_ref, *, mh, bsz):
    c = S5_CHUNK
    d = pl.program_id(1)

    @pl.when(pl.program_id(2) == 0)
    def _():
        _s5_build_operators(d, bb_ref, cc_ref, prow_ref, pcol_ref, t_ref, f_ref, e_ref)
        xr_ref[...] = jnp.zeros_like(xr_ref)
        xi_ref[...] = jnp.zeros_like(xi_ref)

    for j in range(c):
        lhs_ref[:, j * 128 : (j + 1) * 128] = u_ref[pl.ds(j, mh, stride=c), :].astype(bf16)
    lhs = lhs_ref[...]
    y = jnp.dot(lhs, t_ref[...], preferred_element_type=f32)
    vr_ref[...] = jnp.dot(lhs, f_ref[:, :S5_ST], preferred_element_type=f32)
    vi_ref[...] = jnp.dot(lhs, f_ref[:, S5_ST:], preferred_element_type=f32)
    ar = prow_ref[0, c : c + 1, :]
    ai = prow_ref[1, c : c + 1, :]
    n_loc = mh // bsz

    def body(n, carry):
        xr, xi = carry
        ci = jnp.where(d == 0, n, n_loc - 1 - n)
        rows = pl.ds(pl.multiple_of(ci * bsz, bsz), bsz)
        vr = vr_ref[rows, :]
        vi = vi_ref[rows, :]
        vr_ref[rows, :] = xr
        vi_ref[rows, :] = xi
        return ar * xr - ai * xi + vr, ar * xi + ai * xr + vi

    xr, xi = lax.fori_loop(0, n_loc, body, (xr_ref[...], xi_ref[...]))
    xr_ref[...] = xr
    xi_ref[...] = xi
    y = y + jnp.dot(vr_ref[...].astype(bf16), e_ref[:S5_ST, :], preferred_element_type=f32)
    y = y + jnp.dot(vi_ref[...].astype(bf16), e_ref[S5_ST:, :], preferred_element_type=f32)
    for i in range(c):
        y_ref[pl.ds(i, mh, stride=c), :] = y[:, i * 128 : (i + 1) * 128]


def _s5_core(u4, bb, cc, prow, pcol):
    n_chunks, bsz, c, _ = u4.shape
    rows = n_chunks * bsz * c
    n_part = 4 if n_chunks % 4 == 0 else 1
    rp = rows // n_part
    mh = rp // c
    part = lambda d, h: h + d * (n_part - 1 - 2 * h)
    per_tile = lambda a: pl.BlockSpec((None, None) + a.shape[2:], lambda s, d, h: (d, s) + (0,) * (a.ndim - 2))
    return pl.pallas_call(
        functools.partial(_s5_kernel, mh=mh, bsz=bsz),
        out_shape=jax.ShapeDtypeStruct((2, rows, MIX), f32),
        grid=(S5_NSG, 2, n_part),
        in_specs=[
            pl.BlockSpec((rp, 128), lambda s, d, h: (part(d, h), s)),
            per_tile(bb), per_tile(cc), per_tile(prow), per_tile(pcol),
        ],
        out_specs=pl.BlockSpec((None, rp, 128), lambda s, d, h: (d, part(d, h), s)),
        scratch_shapes=[
            pltpu.VMEM((S5_W, S5_W), bf16),
            pltpu.VMEM((S5_W, 2 * S5_ST), bf16),
            pltpu.VMEM((2 * S5_ST, S5_W), bf16),
            pltpu.VMEM((mh, S5_W), bf16),
            pltpu.VMEM((mh, S5_ST), f32),
            pltpu.VMEM((mh, S5_ST), f32),
            pltpu.VMEM((bsz, S5_ST), f32),
            pltpu.VMEM((bsz, S5_ST), f32),
        ],
        compiler_params=_cparams(3),
        name="s5_core",
    )(u4.reshape(rows, MIX), bb, cc, prow, pcol)


def _s5_post_kernel(yf_ref, yb_ref, u_ref, d_ref, w_ref, b_ref, o_ref):
    n = o_ref.shape[0]
    y = (yf_ref[...] + yb_ref[...] + d_ref[...] * u_ref[...]).reshape(n, MIX)
    y = 0.5 * y * (1.0 + jnp.tanh(math.sqrt(2.0 / math.pi) * (y + 0.044715 * (y * y * y))))
    o_ref[...] = (y * _sigmoid(_mm(y, w_ref[...]) + b_ref[...])).astype(o_ref.dtype)


def _s5_post(y2, u4, d_skip, glu_w, glu_b):
    n_chunks, bsz, c, _ = u4.shape
    seq = n_chunks * c
    tok = min(S5_TOK, seq)
    nck = tok // c
    y5 = y2.reshape(2, n_chunks, bsz, c, MIX)
    row = pl.BlockSpec((1, MIX), lambda b, i: (0, 0))
    return pl.pallas_call(
        _s5_post_kernel,
        out_shape=jax.ShapeDtypeStruct((bsz * seq, MIX), bf16),
        grid=(bsz, seq // tok),
        in_specs=[
            pl.BlockSpec((None, nck, None, c, MIX), lambda b, i: (0, i, b, 0, 0)),
            pl.BlockSpec((None, nck, None, c, MIX), lambda b, i: (1, i, b, 0, 0)),
            pl.BlockSpec((nck, None, c, MIX), lambda b, i: (i, b, 0, 0)),
            row,
            pl.BlockSpec((MIX, MIX), lambda b, i: (0, 0)),
            row,
        ],
        out_specs=pl.BlockSpec((tok, MIX), lambda b, i: (b * (seq // tok) + i, 0)),
        compiler_params=_cparams(2),
        name="s5_post",
    )(y5, y5, u4, d_skip, glu_w, glu_b)


def _s5_tables(a_re, a_im, log_dt, b_re, b_im, c_re, c_im):
    c = S5_CHUNK
    dt = jnp.exp(log_dt)[..., None]
    mag = a_re * dt
    ph = a_im * dt
    steps = jnp.arange(c + 1, dtype=f32)[:, None, None, None]
    m = jnp.exp(mag[None] * steps)
    pr = m * jnp.cos(ph[None] * steps)
    pi = m * jnp.sin(ph[None] * steps)
    den = a_re * a_re + a_im * a_im
    qr = ((pr[1] - 1.0) * a_re + pi[1] * a_im) / den
    qi = (pi[1] * a_re - (pr[1] - 1.0) * a_im) / den
    bbr = (qr[..., None] * b_re - qi[..., None] * b_im).transpose(0, 1, 3, 2)
    bbi = (qr[..., None] * b_im + qi[..., None] * b_re).transpose(0, 1, 3, 2)
    bb = jnp.stack([bbr, bbi], axis=1).reshape(2, 2, S5_NSG, S5_SG * S5_CG, S5_P).transpose(0, 2, 1, 3, 4)
    cc = jnp.stack([c_re, c_im], axis=0).transpose(0, 1, 3, 2).reshape(2, S5_NSG, S5_ST, S5_CG)
    cc = jnp.broadcast_to(cc.transpose(1, 0, 2, 3)[None], (2, S5_NSG, 2, S5_ST, S5_CG))
    d_idx = jnp.arange(2)[:, None]
    steps_i = jnp.arange(c)
    fsel = jnp.stack([c - 1 - steps_i, steps_i], axis=0)
    fsel = jnp.concatenate([fsel, jnp.full((2, 1), c)], axis=1)
    tab = lambda p, sel: p[sel, d_idx].reshape(2, sel.shape[1], S5_NSG, S5_ST)
    prow = jnp.stack([tab(pr, fsel), tab(pi, fsel)], axis=1).transpose(0, 3, 1, 2, 4)
    prow = jnp.concatenate([prow, jnp.zeros((2, S5_NSG, 2, 24 - c - 1, S5_ST), f32)], axis=3)
    esel = jnp.stack([steps_i + 1, c - steps_i], axis=0)
    esel = jnp.concatenate([esel, jnp.broadcast_to(steps_i, (2, c))], axis=1)
    pcol = jnp.stack([tab(pr, esel), tab(pi, esel)], axis=1).transpose(0, 3, 1, 4, 2)
    return bb, cc, prow, pcol


def _block_diag_ones(block, scale=1.0):
    idx = jnp.arange(MIX) // block
    return ((idx[:, None] == idx[None, :]).astype(f32) * scale).astype(bf16)


def kernel(x, p, ffn1_norm, ffn1_w1, ffn1_w3, ffn1_w2, mix_norm, w_in, rwkv_mu, rwkv_w0, rwkv_w2, rwkv_a0, rwkv_a2, rwkv_g2, rwkv_k_k, rwkv_k_a, rwkv_r_k, rwkv_ln_w, rwkv_ln_b, dn_conv, dn_a_log, dn_dt_bias, dn_norm, s5_a_re, s5_a_im, s5_log_dt, s5_b_re, s5_b_im, s5_c_re, s5_c_im, s5_d, s5_glu_w, s5_glu_b, w_branch_a, w_branch_b, w_branch_c, w_out, ffn2_norm, ffn2_w1, ffn2_w3, ffn2_w2, ple_norm, ple_w_gate, ple_w_proj, final_norm):
    bsz, seq, _ = x.shape
    t = bsz * seq
    depth = p.shape[0]
    row = lambda a: a.reshape(1, -1).astype(f32)
    ones64 = _block_diag_ones(RW_HD)
    avg64 = _block_diag_ones(RW_HD, 1.0 / RW_HD)
    ones128 = _block_diag_ones(DN_HD)
    avg128 = _block_diag_ones(DN_HD, 1.0 / DN_HD)
    o_rw, o_dn, o_s5, o_gate = 0, 3488, 3488 + 4128, 3488 + 4128 + 1024

    xt = x.reshape(t, D_MODEL)
    for i in range(depth):
        xt = _ffn(xt, row(ffn1_norm[i]), ffn1_w1[i].astype(bf16), ffn1_w3[i].astype(bf16), ffn1_w2[i].astype(bf16))

        wi = w_in[i]
        g_mix = row(mix_norm[i])
        w_all = jnp.concatenate([
            wi[:, o_rw : o_rw + 3488], jnp.zeros((D_MODEL, PROJ_DNS - 3488), f32),
            wi[:, o_dn + 4 * MIX : o_dn + 4128], jnp.zeros((D_MODEL, PROJ_DN - PROJ_DNS - 4 * DN_HEADS), f32),
            wi[:, o_dn : o_dn + 4 * MIX]], axis=1).astype(bf16)
        w_s5 = wi[:, o_s5 : o_s5 + MIX].astype(bf16)
        w_gate = wi[:, o_gate:].astype(bf16)
        c_all = _norm_mm(xt, g_mix, w_all).reshape(bsz, seq, PROJ_COLS)

        mu = jnp.concatenate([rwkv_mu[i], jnp.zeros((RW_COLS - 3488,), f32)]).reshape(1, RW_COLS)
        zeros_blk = jnp.zeros((64, MIX), f32)
        w2cat = jnp.concatenate([
            jnp.concatenate([rwkv_w2[i, 0], zeros_blk], axis=1),
            jnp.concatenate([zeros_blk, rwkv_w2[i, 1]], axis=1)], axis=0).astype(bf16)
        a2cat = jnp.concatenate([
            jnp.concatenate([rwkv_a2[i, 0], zeros_blk], axis=1),
            jnp.concatenate([zeros_blk, rwkv_a2[i, 1]], axis=1)], axis=0).astype(bf16)
        g2pad = jnp.concatenate([rwkv_g2[i], jnp.zeros((256 - 160, MIX), f32)], axis=0).astype(bf16)
        prep = _rwkv_prep(c_all, mu, row(rwkv_w0[i]), row(rwkv_a0[i]), w2cat, a2cat, g2pad,
                          row(rwkv_k_k[i]), row(rwkv_k_a[i]), row(rwkv_r_k[i]), ones64)
        r_, v_, an_, lwf_, lwb_, kf_, kb_, bf_, bb_, gate_, bonus_ = prep
        yf_rw, yb_rw = _rwkv_scan(r_, v_, an_, lwf_, lwb_, kf_, kb_, bf_, bb_)
        ya = _rwkv_post(yf_rw.reshape(t, MIX), yb_rw.reshape(t, MIX), bonus_.reshape(t, MIX), gate_.reshape(t, MIX),
                        row(rwkv_ln_w[i]), row(rwkv_ln_b[i]), avg64)

        conv8 = jnp.concatenate([dn_conv[i], jnp.zeros((8 - DN_CONV, 3 * MIX), f32)], axis=0)
        q_, k_, v2_ = _dn_prep(c_all, conv8, ones128)
        pad16 = lambda a: jnp.concatenate([jnp.zeros((2 * DN_HEADS,), f32), a.reshape(-1), jnp.zeros((128 - 4 * DN_HEADS,), f32)]).reshape(1, 128)
        of_dn, ob_dn = _dn_scan(q_, k_, v2_, c_all, pad16(dn_a_log[i]), pad16(dn_dt_bias[i]))
        yb = _dn_post(of_dn.reshape(t, MIX), ob_dn.reshape(t, MIX), c_all.reshape(t, PROJ_COLS), jnp.tile(dn_norm[i], DN_HEADS).reshape(1, MIX), avg128)

        u4 = _s5_proj(xt.reshape(bsz, seq, D_MODEL), g_mix, w_s5)
        ops5 = _s5_tables(s5_a_re[i], s5_a_im[i], s5_log_dt[i], s5_b_re[i], s5_b_im[i], s5_c_re[i], s5_c_im[i])
        yc = _s5_post(_s5_core(u4, *ops5), u4, row(s5_d[i]), s5_glu_w[i].astype(bf16), row(s5_glu_b[i]))

        xt = _merge(xt, g_mix, ya, yb, yc, w_gate, w_branch_a[i].astype(bf16), w_branch_b[i].astype(bf16),
                    w_branch_c[i].astype(bf16), w_out[i].astype(bf16))
        xt = _ffn(xt, row(ffn2_norm[i]), ffn2_w1[i].astype(bf16), ffn2_w3[i].astype(bf16), ffn2_w2[i].astype(bf16))
        xt = _ple(xt, p[i].reshape(t, PLE_DIM), row(ple_norm[i]), ple_w_gate[i].astype(bf16), ple_w_proj[i].astype(bf16))
    return _final_norm(xt, row(final_norm)).reshape(bsz, seq, D_MODEL)
```

```python
import functools
import math

import jax
import jax.numpy as jnp
from jax import lax
from jax.experimental import pallas as pl
from jax.experimental.pallas import tpu as pltpu

f32 = jnp.float32
bf16 = jnp.bfloat16

D_MODEL = 2048
D_FF = 5632
PLE_DIM = 256
NORM_EPS = 1e-6
MIX = 1024

RW_HEADS = 16
RW_HD = 64
RW_DECAY_SCALE = math.exp(-0.5)
RW_GN_EPS = 64e-5
RW_COLS = 3584
RW_CHUNK = 64

DN_HEADS = 8
DN_HD = 128
DN_CONV = 5
DN_CHUNK = 64

S5_G = 64
S5_CG = 16
S5_P = 64
S5_CHUNK = 16

PROJ_RW = 0
PROJ_DNS = 3584
PROJ_DN = 4096
PROJ_COLS = 8192

VMEM_LIMIT = 56 * 1024 * 1024


def _cparams(n_axes):
    return pltpu.CompilerParams(
        dimension_semantics=("arbitrary",) * n_axes, vmem_limit_bytes=VMEM_LIMIT
    )


def _mm(a, b):
    return jnp.dot(a.astype(bf16), b.astype(bf16), preferred_element_type=f32)


def _bmm(a, b):
    return lax.dot_general(a.astype(bf16), b.astype(bf16), (((2,), (1,)), ((0,), (0,))), preferred_element_type=f32)


def _bmm_nt(a, b):
    return lax.dot_general(a.astype(bf16), b.astype(bf16), (((2,), (2,)), ((0,), (0,))), preferred_element_type=f32)


def _bmm_tn(a, b):
    return lax.dot_general(a.astype(bf16), b.astype(bf16), (((1,), (1,)), ((0,), (0,))), preferred_element_type=f32)


def _mm2(a, b01):
    hi = a.astype(bf16)
    lo = (a - hi.astype(f32)).astype(bf16)
    return jnp.dot(hi, b01, preferred_element_type=f32) + jnp.dot(lo, b01, preferred_element_type=f32)


def _mm3(m01, x):
    hi = x.astype(bf16)
    r1 = x - hi.astype(f32)
    mid = r1.astype(bf16)
    lo = (r1 - mid.astype(f32)).astype(bf16)
    dot = lambda p: jnp.dot(m01, p, preferred_element_type=f32)
    return dot(hi) + dot(mid) + dot(lo)


def _rms(x, g):
    return x * lax.rsqrt(jnp.mean(x * x, axis=-1, keepdims=True) + NORM_EPS) * g


def _sigmoid(x):
    return 1.0 / (1.0 + jnp.exp(-x))


def _silu(x):
    return x * _sigmoid(x)


def _softplus(x):
    return jnp.maximum(x, 0.0) + jnp.log(1.0 + jnp.exp(-jnp.abs(x)))


def _ffn_kernel(x_ref, g_ref, w1_ref, w3_ref, w2_ref, o_ref, h_ref):
    @pl.when(pl.program_id(1) == 0)
    def _():
        x = x_ref[...]
        h_ref[...] = _rms(x, g_ref[...]).astype(bf16)
        o_ref[...] = x

    h = h_ref[...]
    a = jnp.dot(h, w1_ref[...], preferred_element_type=f32)
    b = jnp.dot(h, w3_ref[...], preferred_element_type=f32)
    act = (0.5 * _silu(a) * b).astype(bf16)
    o_ref[...] += jnp.dot(act, w2_ref[...], preferred_element_type=f32)


def _ffn(x, g, w1, w3, w2, tm=512, tf=512):
    t = x.shape[0]
    tm = min(tm, t)
    return pl.pallas_call(
        _ffn_kernel,
        out_shape=jax.ShapeDtypeStruct((t, D_MODEL), f32),
        grid=(t // tm, D_FF // tf),
        in_specs=[
            pl.BlockSpec((tm, D_MODEL), lambda i, j: (i, 0)),
            pl.BlockSpec((1, D_MODEL), lambda i, j: (0, 0)),
            pl.BlockSpec((D_MODEL, tf), lambda i, j: (0, j)),
            pl.BlockSpec((D_MODEL, tf), lambda i, j: (0, j)),
            pl.BlockSpec((tf, D_MODEL), lambda i, j: (j, 0)),
        ],
        out_specs=pl.BlockSpec((tm, D_MODEL), lambda i, j: (i, 0)),
        scratch_shapes=[pltpu.VMEM((tm, D_MODEL), bf16)],
        compiler_params=_cparams(2),
        name="ffn",
    )(x, g, w1, w3, w2)


def _norm_mm_kernel(x_ref, g_ref, w_ref, o_ref, h_ref):
    @pl.when(pl.program_id(1) == 0)
    def _():
        h_ref[...] = _rms(x_ref[...], g_ref[...]).astype(bf16)

    o_ref[...] = jnp.dot(h_ref[...], w_ref[...], preferred_element_type=f32)


def _norm_mm(x, g, w, tm=1024, tn=512):
    t = x.shape[0]
    n = w.shape[1]
    tm = min(tm, t)
    return pl.pallas_call(
        _norm_mm_kernel,
        out_shape=jax.ShapeDtypeStruct((t, n), f32),
        grid=(t // tm, n // tn),
        in_specs=[
            pl.BlockSpec((tm, D_MODEL), lambda i, j: (i, 0)),
            pl.BlockSpec((1, D_MODEL), lambda i, j: (0, 0)),
            pl.BlockSpec((D_MODEL, tn), lambda i, j: (0, j)),
        ],
        out_specs=pl.BlockSpec((tm, tn), lambda i, j: (i, j)),
        scratch_shapes=[pltpu.VMEM((tm, D_MODEL), bf16)],
        compiler_params=_cparams(2),
        name="norm_mm",
    )(x, g, w)


def _merge_kernel(x_ref, g_ref, ya_ref, yb_ref, yc_ref, wga_ref, wgb_ref, wgc_ref, wa_ref, wb_ref, wc_ref, wo_ref, o_ref, h_ref):
    @pl.when(pl.program_id(1) == 0)
    def _():
        x = x_ref[...]
        h_ref[...] = _rms(x, g_ref[...]).astype(bf16)
        o_ref[...] = x

    h = h_ref[...]
    m = None
    for y_ref, wg_ref, w_ref in ((ya_ref, wga_ref, wa_ref), (yb_ref, wgb_ref, wb_ref), (yc_ref, wgc_ref, wc_ref)):
        gate = _sigmoid(jnp.dot(h, wg_ref[...], preferred_element_type=f32))
        term = gate * jnp.dot(y_ref[...], w_ref[...], preferred_element_type=f32)
        m = term if m is None else m + term
    o_ref[...] += jnp.dot(m.astype(bf16), wo_ref[...], preferred_element_type=f32)


def _merge(x, g, ya, yb, yc, w_gate, wa, wb, wc, wo, tm=512, tn=512):
    t = x.shape[0]
    tm = min(tm, t)
    nj = D_MODEL // tn
    tok = pl.BlockSpec((tm, MIX), lambda i, j: (i, 0))
    gate_w = lambda b: pl.BlockSpec((D_MODEL, tn), lambda i, j: (0, b * nj + j))
    br_w = pl.BlockSpec((MIX, tn), lambda i, j: (0, j))
    return pl.pallas_call(
        _merge_kernel,
        out_shape=jax.ShapeDtypeStruct((t, D_MODEL), f32),
        grid=(t // tm, nj),
        in_specs=[
            pl.BlockSpec((tm, D_MODEL), lambda i, j: (i, 0)),
            pl.BlockSpec((1, D_MODEL), lambda i, j: (0, 0)),
            tok, tok, tok,
            gate_w(0), gate_w(1), gate_w(2),
            br_w, br_w, br_w,
            pl.BlockSpec((tn, D_MODEL), lambda i, j: (j, 0)),
        ],
        out_specs=pl.BlockSpec((tm, D_MODEL), lambda i, j: (i, 0)),
        scratch_shapes=[pltpu.VMEM((tm, D_MODEL), bf16)],
        compiler_params=_cparams(2),
        name="merge",
    )(x, g, ya, yb, yc, w_gate, w_gate, w_gate, wa, wb, wc, wo)


def _ple_kernel(x_ref, xj_ref, p_ref, g_ref, wg_ref, wp_ref, o_ref, h_ref):
    @pl.when(pl.program_id(1) == 0)
    def _():
        h_ref[...] = _rms(x_ref[...], g_ref[...]).astype(bf16)

    gate = _sigmoid(jnp.dot(h_ref[...], wg_ref[...], preferred_element_type=f32))
    emb = jnp.dot(p_ref[...].astype(bf16), wp_ref[...], preferred_element_type=f32)
    o_ref[...] = xj_ref[...] + emb * gate


def _ple(x, p, g, wg, wp, tm=512, tn=1024):
    t = x.shape[0]
    tm = min(tm, t)
    return pl.pallas_call(
        _ple_kernel,
        out_shape=jax.ShapeDtypeStruct((t, D_MODEL), f32),
        grid=(t // tm, D_MODEL // tn),
        in_specs=[
            pl.BlockSpec((tm, D_MODEL), lambda i, j: (i, 0)),
            pl.BlockSpec((tm, tn), lambda i, j: (i, j)),
            pl.BlockSpec((tm, PLE_DIM), lambda i, j: (i, 0)),
            pl.BlockSpec((1, D_MODEL), lambda i, j: (0, 0)),
            pl.BlockSpec((D_MODEL, tn), lambda i, j: (0, j)),
            pl.BlockSpec((PLE_DIM, tn), lambda i, j: (0, j)),
        ],
        out_specs=pl.BlockSpec((tm, tn), lambda i, j: (i, j)),
        scratch_shapes=[pltpu.VMEM((tm, D_MODEL), bf16)],
        compiler_params=_cparams(2),
        name="ple",
    )(x, x, p, g, wg, wp)


def _final_norm_kernel(x_ref, g_ref, o_ref):
    o_ref[...] = _rms(x_ref[...], g_ref[...])


def _final_norm(x, g, tm=512):
    t = x.shape[0]
    tm = min(tm, t)
    return pl.pallas_call(
        _final_norm_kernel,
        out_shape=jax.ShapeDtypeStruct((t, D_MODEL), f32),
        grid=(t // tm,),
        in_specs=[
            pl.BlockSpec((tm, D_MODEL), lambda i: (i, 0)),
            pl.BlockSpec((1, D_MODEL), lambda i: (0, 0)),
        ],
        out_specs=pl.BlockSpec((tm, D_MODEL), lambda i: (i, 0)),
        compiler_params=_cparams(1),
        name="final_norm",
    )(x, g)


def _halo(prev_ref, c_ref, next_ref, n_tiles):
    i = pl.program_id(1)
    prev = jnp.where(i == 0, 0.0, prev_ref[...])
    nxt = jnp.where(i == n_tiles - 1, 0.0, next_ref[...])
    return jnp.concatenate([prev, c_ref[...], nxt], axis=0)


def _shifted(ext, k, tl):
    n = ext.shape[0]
    return pltpu.roll(ext, (-k) % n, axis=0)[8 : 8 + tl]


def _iota2(c):
    return lax.broadcasted_iota(jnp.int32, (c, c), 0), lax.broadcasted_iota(jnp.int32, (c, c), 1)


def _by_dir(x, fn_f, fn_b):
    n = x.shape[0] // 2
    return jnp.concatenate([fn_f(x[:n]), fn_b(x[n:])], axis=0)


def _mask_dir(x, mask_f, mask_b):
    return _by_dir(x, lambda a: jnp.where(mask_f, a, 0.0), lambda a: jnp.where(mask_b, a, 0.0))


def _unit_tri_inverse(m, c):
    ii, jj = _iota2(c)
    same = lambda sh: (ii >> sh) == (jj >> sh)
    d = (ii == jj).astype(f32) + _mask_dir(m, same(1) & (ii > jj), same(1) & (jj > ii))
    sh = 1
    while (1 << sh) < c:
        hi_i = ((ii >> sh) & 1) == 1
        hi_j = ((jj >> sh) & 1) == 1
        ms = _mask_dir(m, same(sh + 1) & hi_i & ~hi_j, same(sh + 1) & hi_j & ~hi_i)
        d = d + _bmm(d, _bmm(ms, d))
        sh += 1
    return d


def _rwkv_prep_kernel(
    prev_ref, c_ref, next_ref, mu_ref, w0_ref, a0_ref, w2_ref, a2_ref, g2_ref, kk_ref, ka_ref, rk_ref, ones_ref,
    r_o, v_o, an_o, lwf_o, lwb_o, kf_o, kb_o, bf_o, bb_o, gate_o, bonus_o, *, tl, n_tiles
):
    ext = _halo(prev_ref, c_ref, next_ref, n_tiles)
    c = ext[8 : 8 + tl]
    cs = c + (0.5 * (_shifted(ext, -1, tl) + _shifted(ext, 1, tl)) - c) * mu_ref[...]
    r = cs[:, 0:MIX]
    k = cs[:, MIX : 2 * MIX]
    v = cs[:, 2 * MIX : 3 * MIX]
    lw_in = jnp.tanh(cs[:, 3 * MIX : 3 * MIX + 128])
    la_in = cs[:, 3 * MIX + 128 : 3 * MIX + 256]
    lg_in = _sigmoid(cs[:, 3 * MIX + 256 : 3 * MIX + 512])
    logw = -RW_DECAY_SCALE * _sigmoid(w0_ref[...] + _mm(lw_in, w2_ref[...]))
    a = _sigmoid(a0_ref[...] + _mm(la_in, a2_ref[...]))
    gate_o[...] = _mm(lg_in, g2_ref[...]).astype(gate_o.dtype)
    ones = ones_ref[...]
    kx = k * kk_ref[...]
    kk = kx * lax.rsqrt(_mm2(kx * kx, ones) + 1e-6)
    a_f = a[:, 0:MIX]
    a_b = a[:, MIX : 2 * MIX]
    ka = ka_ref[...]
    k_f = k * (1.0 + (a_f - 1.0) * ka)
    k_b = k * (1.0 + (a_b - 1.0) * ka)
    r_o[...] = r.astype(r_o.dtype)
    v_o[...] = v.astype(v_o.dtype)
    an_o[...] = (-kk).astype(an_o.dtype)
    lwf_o[...] = logw[:, 0:MIX]
    lwb_o[...] = logw[:, MIX : 2 * MIX]
    kf_o[...] = k_f.astype(kf_o.dtype)
    kb_o[...] = k_b.astype(kb_o.dtype)
    bf_o[...] = (kk * a_f).astype(bf_o.dtype)
    bb_o[...] = (kk * a_b).astype(bb_o.dtype)
    bonus_o[...] = (_mm2(r * (k_f + k_b) * rk_ref[...], ones) * v).astype(bonus_o.dtype)


def _rwkv_prep(c, mu, w0, a0, w2, a2, g2, k_k, k_a, r_k, ones64, tl=128):
    bsz, seq, _ = c.shape
    tl = min(tl, seq)
    n_tiles = seq // tl
    nb = tl // 8
    row = lambda a: pl.BlockSpec(a.shape, lambda b, i: (0,) * a.ndim)
    outs = [jax.ShapeDtypeStruct((bsz, seq, MIX), f32 if n in (3, 4) else bf16) for n in range(11)]
    ospec = pl.BlockSpec((None, tl, MIX), lambda b, i: (b, i, 0))
    return pl.pallas_call(
        functools.partial(_rwkv_prep_kernel, tl=tl, n_tiles=n_tiles),
        out_shape=outs,
        grid=(bsz, n_tiles),
        in_specs=[
            pl.BlockSpec((None, 8, RW_COLS), lambda b, i: (b, jnp.maximum(i * nb - 1, 0), 0)),
            pl.BlockSpec((None, tl, RW_COLS), lambda b, i: (b, i, 0)),
            pl.BlockSpec((None, 8, RW_COLS), lambda b, i: (b, jnp.minimum((i + 1) * nb, seq // 8 - 1), 0)),
            row(mu), row(w0), row(a0), row(w2), row(a2), row(g2), row(k_k), row(k_a), row(r_k), row(ones64),
        ],
        out_specs=[ospec] * 11,
        compiler_params=_cparams(2),
        name="rwkv_prep",
    )(c, c, c, mu, w0, a0, w2, a2, g2, k_k, k_a, r_k, ones64)


def _pair_diag(x):
    lo = lax.broadcasted_iota(jnp.int32, x.shape, 2) < RW_HD
    zero = jnp.zeros_like(x)
    return jnp.concatenate([jnp.where(lo, x, zero), jnp.where(lo, zero, x)], axis=1)


def _pair_tri_inverse(m, ii, jj):
    same = lambda sh: (ii >> sh) == (jj >> sh)
    d = (ii == jj).astype(f32) + _mask_dir(m, same(1) & (ii > jj), same(1) & (jj > ii))
    sh = 1
    while (1 << sh) < RW_CHUNK:
        hi_i = ((ii >> sh) & 1) == 1
        hi_j = ((jj >> sh) & 1) == 1
        ms = _mask_dir(m, same(sh + 1) & hi_i & ~hi_j, same(sh + 1) & hi_j & ~hi_i)
        inner = _bmm(ms, _pair_diag(d.astype(bf16)))
        d = d + _bmm(d, _pair_diag(inner.astype(bf16)))
        sh += 1
    return d


def _rwkv_chunks(s, at, rt, bt, kt, v, bh, kh, e_tot):
    c = RW_CHUNK
    ii = lax.broadcasted_iota(jnp.int32, (c, 2 * c), 0)
    jj = lax.broadcasted_iota(jnp.int32, (c, 2 * c), 1) & (c - 1)
    ar = jnp.concatenate([at, rt], axis=1)
    rb = _bmm_nt(ar, _pair_diag(bt))
    rk = _bmm_nt(ar, _pair_diag(kt))
    a_ab = _mask_dir(rb[:, :c], jj < ii, jj > ii)
    a_ak = _mask_dir(rk[:, :c], jj < ii, jj > ii)
    p_b = _mask_dir(rb[:, c:], jj <= ii, jj >= ii)
    p_k = _mask_dir(rk[:, c:], jj <= ii, jj >= ii)
    tinv = _pair_tri_inverse(a_ab, ii, jj)
    v_d = _pair_diag(v)
    akv = _bmm(a_ak, v_d)
    wu = _bmm(tinv, jnp.concatenate([_pair_diag(at), _pair_diag(akv.astype(bf16))], axis=2))
    w = wu[:, :, :128]
    ws = _bmm_nt(jnp.concatenate([w.astype(bf16), rt], axis=1), s)
    u = (ws[:, :c] + wu[:, :, 128:]).astype(bf16)
    y = ws[:, c:] + _bmm(jnp.concatenate([p_b, p_k], axis=2), jnp.concatenate([_pair_diag(u), v_d], axis=1))
    upd = _bmm_tn(jnp.concatenate([u, v], axis=1), jnp.concatenate([bh, kh], axis=1))
    r2 = lax.broadcasted_iota(jnp.int32, (2 * c, 2 * c), 0) >> 6
    c2 = lax.broadcasted_iota(jnp.int32, (2 * c, 2 * c), 1) >> 6
    s_new = s * e_tot + jnp.where(r2 == c2, upd, 0.0)
    return y, s_new


def _rwkv_scan_kernel(rf, vf, anf, lwf, kf, bf_, rb, vb, anb, lwb, kb, bb, yf_ref, yb_ref, s_ref):
    c = RW_CHUNK

    @pl.when(pl.program_id(1) == 0)
    def _():
        s_ref[...] = jnp.zeros_like(s_ref)

    ii, jj = _iota2(c)

    def side(r_ref, v_ref, an_ref, lw_ref, k_ref, b_ref, cum, last):
        lw = lw_ref[...]
        cl = _mm3(cum.astype(bf16), lw)
        e_neg = jnp.exp(-cl)
        cl_last = cl[last : last + 1, :]
        e_last = jnp.exp(cl_last - cl)
        b = b_ref[...].astype(f32)
        kd = k_ref[...].astype(f32)
        parts = (an_ref[...].astype(f32) * jnp.exp(cl - lw), r_ref[...].astype(f32) * jnp.exp(cl), b * e_neg, kd * e_neg, v_ref[...], b * e_last, kd * e_last)
        return [p.astype(bf16) for p in parts], jnp.exp(cl_last)

    pf, ef = side(rf, vf, anf, lwf, kf, bf_, jj <= ii, c - 1)
    pb, eb = side(rb, vb, anb, lwb, kb, bb, jj >= ii, 0)
    n_pairs = RW_HEADS // 2
    pairs = lambda x: [x[:, p * 128 : (p + 1) * 128] for p in range(n_pairs)]
    ops = [jnp.stack(pairs(a) + pairs(b), axis=0) for a, b in zip(pf, pb)]
    e_tot = jnp.stack(pairs(ef) + pairs(eb), axis=0)
    y, s_new = _rwkv_chunks(s_ref[...], *ops, e_tot)
    s_ref[...] = s_new
    for p in range(n_pairs):
        yf_ref[:, p * 128 : (p + 1) * 128] = y[p].astype(yf_ref.dtype)
        yb_ref[:, p * 128 : (p + 1) * 128] = y[n_pairs + p].astype(yb_ref.dtype)


def _rwkv_scan(r, v, an, lwf, lwb, kf, kb, bf_, bb):
    bsz, seq, _ = r.shape
    n_chunks = seq // RW_CHUNK
    fwd = pl.BlockSpec((None, RW_CHUNK, MIX), lambda b, i: (b, i, 0))
    bwd = pl.BlockSpec((None, RW_CHUNK, MIX), lambda b, i: (b, n_chunks - 1 - i, 0))
    out = jax.ShapeDtypeStruct((bsz, seq, MIX), bf16)
    return pl.pallas_call(
        _rwkv_scan_kernel,
        out_shape=[out, out],
        grid=(bsz, n_chunks),
        in_specs=[fwd] * 6 + [bwd] * 6,
        out_specs=[fwd, bwd],
        scratch_shapes=[pltpu.VMEM((RW_HEADS, 2 * RW_HD, 2 * RW_HD), f32)],
        compiler_params=_cparams(2),
        name="rwkv_scan",
    )(r, v, an, lwf, kf, bf_, r, v, an, lwb, kb, bb)


def _rwkv_post_kernel(yf_ref, yb_ref, bonus_ref, gate_ref, lnw_ref, lnb_ref, avg_ref, o_ref):
    y = yf_ref[...].astype(f32) + yb_ref[...].astype(f32)
    avg = avg_ref[...]
    mean = _mm2(y, avg)
    yc = y - mean
    var = _mm2(yc * yc, avg)
    yn = yc * lax.rsqrt(var + RW_GN_EPS) * lnw_ref[...] + lnb_ref[...]
    o_ref[...] = ((yn + bonus_ref[...].astype(f32)) * gate_ref[...].astype(f32)).astype(o_ref.dtype)


def _rwkv_post(yf, yb, bonus, gate, ln_w, ln_b, avg64, tm=512):
    t = yf.shape[0]
    tm = min(tm, t)
    tok = pl.BlockSpec((tm, MIX), lambda i: (i, 0))
    row = pl.BlockSpec((1, MIX), lambda i: (0, 0))
    return pl.pallas_call(
        _rwkv_post_kernel,
        out_shape=jax.ShapeDtypeStruct((t, MIX), bf16),
        grid=(t // tm,),
        in_specs=[tok, tok, tok, tok, row, row, pl.BlockSpec((MIX, MIX), lambda i: (0, 0))],
        out_specs=tok,
        compiler_params=_cparams(1),
        name="rwkv_post",
    )(yf, yb, bonus, gate, ln_w, ln_b, avg64)


def _dn_prep_kernel(*refs, tl, n_tiles):
    halos, (cw_ref, ones_ref, q_o, k_o, v_o) = refs[:9], refs[9:]
    cw = cw_ref[...]
    parts = []
    for p in range(3):
        ext = _halo(*halos[3 * p : 3 * p + 3], n_tiles)
        acc = None
        for j in range(DN_CONV):
            term = _shifted(ext, j - DN_CONV // 2, tl) * cw[j : j + 1, p * MIX : (p + 1) * MIX]
            acc = term if acc is None else acc + term
        parts.append(_silu(acc))
    q, k, v = parts
    ones = ones_ref[...]
    q_o[...] = (q * lax.rsqrt(_mm2(q * q, ones) + 1e-6) * (DN_HD ** -0.5)).astype(q_o.dtype)
    k_o[...] = (k * lax.rsqrt(_mm2(k * k, ones) + 1e-6)).astype(k_o.dtype)
    v_o[...] = v.astype(v_o.dtype)


def _dn_prep(c, conv_w8, ones128, tl=128):
    bsz, seq, _ = c.shape
    tl = min(tl, seq)
    n_tiles = seq // tl
    nb = tl // 8
    out = jax.ShapeDtypeStruct((bsz, seq, MIX), bf16)
    ospec = pl.BlockSpec((None, tl, MIX), lambda b, i: (b, i, 0))
    halo_specs = []
    for p in range(3):
        col = PROJ_DN // MIX + p
        halo_specs += [
            pl.BlockSpec((None, 8, MIX), lambda b, i, col=col: (b, jnp.maximum(i * nb - 1, 0), col)),
            pl.BlockSpec((None, tl, MIX), lambda b, i, col=col: (b, i, col)),
            pl.BlockSpec((None, 8, MIX), lambda b, i, col=col: (b, jnp.minimum((i + 1) * nb, seq // 8 - 1), col)),
        ]
    return pl.pallas_call(
        functools.partial(_dn_prep_kernel, tl=tl, n_tiles=n_tiles),
        out_shape=[out] * 3,
        grid=(bsz, n_tiles),
        in_specs=halo_specs + [
            pl.BlockSpec((8, 3 * MIX), lambda b, i: (0, 0)),
            pl.BlockSpec((MIX, MIX), lambda b, i: (0, 0)),
        ],
        out_specs=[ospec] * 3,
        compiler_params=_cparams(2),
        name="dn_prep",
    )(*([c] * 9), conv_w8, ones128)


def _dn_chunks(s, q, k, v, beta, gcol, grow):
    c = DN_CHUNK
    ii, jj = _iota2(c)
    incl_f, incl_b = jj <= ii, jj >= ii
    diff = gcol[:, :, :c] - grow
    decay = _by_dir(
        diff,
        lambda a: jnp.where(incl_f, jnp.exp(jnp.where(incl_f, a, 0.0)), 0.0),
        lambda a: jnp.where(incl_b, jnp.exp(jnp.where(incl_b, a, 0.0)), 0.0),
    )
    kb = k * beta
    r = _bmm_nt(jnp.concatenate([kb, q], axis=1), k)
    nmat = _mask_dir(r[:, :c] * decay, jj < ii, jj > ii)
    qk = r[:, c:] * decay
    tinv = _unit_tri_inverse(-nmat, c)
    e_gc = jnp.exp(gcol)
    uw = _bmm(tinv, jnp.concatenate([v * beta, kb * e_gc], axis=2))
    u = uw[:, :, :DN_HD]
    w = uw[:, :, DN_HD:]
    wq = _bmm(jnp.concatenate([w, q * e_gc], axis=1), s)
    v_new = u - wq[:, :c]
    o = wq[:, c:] + _bmm(qk, v_new)
    gl = _by_dir(gcol, lambda a: a[:, c - 1 : c, :], lambda a: a[:, 0:1, :])
    s_new = s * jnp.exp(gl) + _bmm_tn(k * jnp.exp(gl - gcol), v_new)
    return o, s_new


def _dn_scan_kernel(qf, kf, vf, gf, qb, kb, vb, gb, alog_ref, dtb_ref, of_ref, ob_ref, s_ref):
    c = DN_CHUNK

    @pl.when(pl.program_id(1) == 0)
    def _():
        s_ref[...] = jnp.zeros_like(s_ref)

    ii, jj = _iota2(c)
    qs, ks, vs, betas, gcols, grows = [], [], [], [], [], []
    for d, (q_ref, k_ref, v_ref, g_ref, cum) in enumerate(((qf, kf, vf, gf, jj <= ii), (qb, kb, vb, gb, jj >= ii))):
        x = g_ref[...]
        beta = _sigmoid(x)
        g = -jnp.exp(alog_ref[...]) * _softplus(x + dtb_ref[...])
        gc = _mm3(cum.astype(bf16), g)
        gct = gc.T
        q, k, v = q_ref[...].astype(f32), k_ref[...].astype(f32), v_ref[...].astype(f32)
        for h in range(DN_HEADS):
            col = 2 * DN_HEADS + DN_HEADS * d + h
            bcol = DN_HEADS * d + h
            sl = slice(h * DN_HD, (h + 1) * DN_HD)
            qs.append(q[:, sl])
            ks.append(k[:, sl])
            vs.append(v[:, sl])
            betas.append(jnp.broadcast_to(beta[:, bcol : bcol + 1], (c, DN_HD)))
            gcols.append(jnp.broadcast_to(gc[:, col : col + 1], (c, DN_HD)))
            grows.append(gct[col : col + 1, :])
    st = lambda xs: jnp.stack(xs, axis=0)
    o, s_new = _dn_chunks(s_ref[...], st(qs), st(ks), st(vs), st(betas), st(gcols), st(grows))
    s_ref[...] = s_new
    for h in range(DN_HEADS):
        of_ref[:, h * DN_HD : (h + 1) * DN_HD] = o[h].astype(of_ref.dtype)
        ob_ref[:, h * DN_HD : (h + 1) * DN_HD] = o[DN_HEADS + h].astype(ob_ref.dtype)


def _dn_scan(q, k, v, gates, alog_row, dtb_row):
    bsz, seq, _ = q.shape
    n_chunks = seq // DN_CHUNK
    fwd = lambda w, col=0: pl.BlockSpec((None, DN_CHUNK, w), lambda b, i: (b, i, col))
    bwd = lambda w, col=0: pl.BlockSpec((None, DN_CHUNK, w), lambda b, i: (b, n_chunks - 1 - i, col))
    gcol = PROJ_DNS // 128
    row = pl.BlockSpec((1, 128), lambda b, i: (0, 0))
    out = jax.ShapeDtypeStruct((bsz, seq, MIX), bf16)
    return pl.pallas_call(
        _dn_scan_kernel,
        out_shape=[out, out],
        grid=(bsz, n_chunks),
        in_specs=[fwd(MIX)] * 3 + [fwd(128, gcol)] + [bwd(MIX)] * 3 + [bwd(128, gcol)] + [row, row],
        out_specs=[fwd(MIX), bwd(MIX)],
        scratch_shapes=[pltpu.VMEM((2 * DN_HEADS, DN_HD, DN_HD), f32)],
        compiler_params=_cparams(2),
        name="dn_scan",
    )(q, k, v, gates, q, k, v, gates, alog_row, dtb_row)


def _dn_post_kernel(of_ref, ob_ref, z_ref, nw_ref, avg_ref, o_ref):
    o = of_ref[...].astype(f32) + ob_ref[...].astype(f32)
    o = o * lax.rsqrt(_mm2(o * o, avg_ref[...]) + NORM_EPS) * nw_ref[...]
    o_ref[...] = (o * _silu(z_ref[...])).astype(o_ref.dtype)


def _dn_post(of, ob, c_dn, norm_w, avg128, tm=512):
    t = of.shape[0]
    tm = min(tm, t)
    tok = pl.BlockSpec((tm, MIX), lambda i: (i, 0))
    return pl.pallas_call(
        _dn_post_kernel,
        out_shape=jax.ShapeDtypeStruct((t, MIX), bf16),
        grid=(t // tm,),
        in_specs=[tok, tok, pl.BlockSpec((tm, MIX), lambda i: (i, PROJ_DN // MIX + 3)),
                  pl.BlockSpec((1, MIX), lambda i: (0, 0)), pl.BlockSpec((MIX, MIX), lambda i: (0, 0))],
        out_specs=tok,
        compiler_params=_cparams(1),
        name="dn_post",
    )(of, ob, c_dn, norm_w, avg128)


S5_SG = 8
S5_NSG = S5_G // S5_SG
S5_ST = S5_SG * S5_P
S5_W = S5_CHUNK * 128
S5_TOK = 512


def _s5_proj_kernel(x_ref, g_ref, w_ref, o_ref):
    h = _rms(x_ref[...], g_ref[...]).astype(bf16)
    y = jnp.dot(h, w_ref[...], preferred_element_type=f32)
    o_ref[...] = y.reshape(o_ref.shape)


def _s5_proj(x3, g, w):
    bsz, seq, _ = x3.shape
    tok = min(S5_TOK, seq)
    nck = tok // S5_CHUNK
    return pl.pallas_call(
        _s5_proj_kernel,
        out_shape=jax.ShapeDtypeStruct((seq // S5_CHUNK, bsz, S5_CHUNK, MIX), f32),
        grid=(bsz, seq // tok),
        in_specs=[
            pl.BlockSpec((None, tok, D_MODEL), lambda b, i: (b, i, 0)),
            pl.BlockSpec((1, D_MODEL), lambda b, i: (0, 0)),
            pl.BlockSpec((D_MODEL, MIX), lambda b, i: (0, 0)),
        ],
        out_specs=pl.BlockSpec((nck, None, S5_CHUNK, MIX), lambda b, i: (i, b, 0, 0)),
        compiler_params=_cparams(2),
        name="s5_proj",
    )(x3, g, w)


def _mm3r(a, b01):
    hi = a.astype(bf16)
    r1 = a - hi.astype(f32)
    mid = r1.astype(bf16)
    lo = (r1 - mid.astype(f32)).astype(bf16)
    dot = lambda p: jnp.dot(p, b01, preferred_element_type=f32)
    return dot(hi) + dot(mid) + dot(lo)


def _s5_build_operators(d, bb_ref, cc_ref, prow_ref, pcol_ref, t_ref, f_ref, e_ref):
    c = S5_CHUNK
    io = lambda shape, ax: lax.broadcasted_iota(jnp.int32, shape, ax)
    tile_f = (io((S5_P, S5_ST), 0) == (io((S5_P, S5_ST), 1) & (S5_P - 1))).astype(bf16)
    mask_f = (io((128, S5_ST), 0) >> 4) == (io((128, S5_ST), 1) >> 6)
    bbr = jnp.where(mask_f, _mm3r(bb_ref[0], tile_f), 0.0)
    bbi = jnp.where(mask_f, _mm3r(bb_ref[1], tile_f), 0.0)
    tile_e = (io((S5_CG, 128), 0) == (io((S5_CG, 128), 1) & (S5_CG - 1))).astype(bf16)
    mask_e = (io((S5_ST, 128), 0) >> 6) == (io((S5_ST, 128), 1) >> 4)
    cre = jnp.where(mask_e, _mm3r(cc_ref[0], tile_e), 0.0)
    cim = jnp.where(mask_e, _mm3r(cc_ref[1], tile_e), 0.0)
    for j in range(c):
        pr = prow_ref[0, j : j + 1, :]
        pi = prow_ref[1, j : j + 1, :]
        f_ref[j * 128 : (j + 1) * 128, :S5_ST] = (pr * bbr - pi * bbi).astype(bf16)
        f_ref[j * 128 : (j + 1) * 128, S5_ST:] = (pr * bbi + pi * bbr).astype(bf16)
    pcr = pcol_ref[0]
    pci = pcol_ref[1]

    def c_times_power(col):
        pr = jnp.broadcast_to(pcr[:, col : col + 1], (S5_ST, 128))
        pi = jnp.broadcast_to(pci[:, col : col + 1], (S5_ST, 128))
        return cre * pr - cim * pi, -(cre * pi + cim * pr)

    for i in range(c):
        er, ei = c_times_power(i)
        e_ref[:S5_ST, i * 128 : (i + 1) * 128] = er.astype(bf16)
        e_ref[S5_ST:, i * 128 : (i + 1) * 128] = ei.astype(bf16)
    bb = jnp.concatenate([bbr, bbi], axis=1)
    ks = []
    for lag in range(c):
        er, ei = c_times_power(c + lag)
        ks.append(_mm(bb, jnp.concatenate([er, ei], axis=0)).astype(bf16))
    t_ref[...] = jnp.zeros_like(t_ref)

    @pl.when(d == 0)
    def _():
        for lag in range(c):
            for j in range(c - lag):
                t_ref[j * 128 : (j + 1) * 128, (j + lag) * 128 : (j + lag + 1) * 128] = ks[lag]

    @pl.when(d == 1)
    def _():
        for lag in range(c):
            for j in range(c - lag):
                t_ref[(j + lag) * 128 : (j + lag + 1) * 128, j * 128 : (j + 1) * 128] = ks[lag]


def _s5_kernel(u_ref, bb_ref, cc_ref, prow_ref, pcol_ref, y_ref, t_ref, f_ref, e_ref, lhs_ref, vr_ref, vi_ref, xr_ref, xi_ref, *, mh, bsz):
    c = S5_CHUNK
    d = pl.program_id(1)

    @pl.when(pl.program_id(2) == 0)
    def _():
        _s5_build_operators(d, bb_ref, cc_ref, prow_ref, pcol_ref, t_ref, f_ref, e_ref)
        xr_ref[...] = jnp.zeros_like(xr_ref)
        xi_ref[...] = jnp.zeros_like(xi_ref)

    for j in range(c):
        lhs_ref[:, j * 128 : (j + 1) * 128] = u_ref[pl.ds(j, mh, stride=c), :].astype(bf16)
    lhs = lhs_ref[...]
    y = jnp.dot(lhs, t_ref[...], preferred_element_type=f32)
    vr_ref[...] = jnp.dot(lhs, f_ref[:, :S5_ST], preferred_element_type=f32)
    vi_ref[...] = jnp.dot(lhs, f_ref[:, S5_ST:], preferred_element_type=f32)
    ar = prow_ref[0, c : c + 1, :]
    ai = prow_ref[1, c : c + 1, :]
    n_loc = mh // bsz

    def body(n, carry):
        xr, xi = carry
        ci = jnp.where(d == 0, n, n_loc - 1 - n)
        rows = pl.ds(pl.multiple_of(ci * bsz, bsz), bsz)
        vr = vr_ref[rows, :]
        vi = vi_ref[rows, :]
        vr_ref[rows, :] = xr
        vi_ref[rows, :] = xi
        return ar * xr - ai * xi + vr, ar * xi + ai * xr + vi

    xr, xi = lax.fori_loop(0, n_loc, body, (xr_ref[...], xi_ref[...]))
    xr_ref[...] = xr
    xi_ref[...] = xi
    y = y + jnp.dot(vr_ref[...].astype(bf16), e_ref[:S5_ST, :], preferred_element_type=f32)
    y = y + jnp.dot(vi_ref[...].astype(bf16), e_ref[S5_ST:, :], preferred_element_type=f32)
    for i in range(c):
        y_ref[pl.ds(i, mh, stride=c), :] = y[:, i * 128 : (i + 1) * 128]


def _s5_core(u4, bb, cc, prow, pcol):
    n_chunks, bsz, c, _ = u4.shape
    rows = n_chunks * bsz * c
    n_part = 4 if n_chunks % 4 == 0 else 1
    rp = rows // n_part
    mh = rp // c
    part = lambda d, h: h + d * (n_part - 1 - 2 * h)
    per_tile = lambda a: pl.BlockSpec((None, None) + a.shape[2:], lambda s, d, h: (d, s) + (0,) * (a.ndim - 2))
    return pl.pallas_call(
        functools.partial(_s5_kernel, mh=mh, bsz=bsz),
        out_shape=jax.ShapeDtypeStruct((2, rows, MIX), f32),
        grid=(S5_NSG, 2, n_part),
        in_specs=[
            pl.BlockSpec((rp, 128), lambda s, d, h: (part(d, h), s)),
            per_tile(bb), per_tile(cc), per_tile(prow), per_tile(pcol),
        ],
        out_specs=pl.BlockSpec((None, rp, 128), lambda s, d, h: (d, part(d, h), s)),
        scratch_shapes=[
            pltpu.VMEM((S5_W, S5_W), bf16),
            pltpu.VMEM((S5_W, 2 * S5_ST), bf16),
            pltpu.VMEM((2 * S5_ST, S5_W), bf16),
            pltpu.VMEM((mh, S5_W), bf16),
            pltpu.VMEM((mh, S5_ST), f32),
            pltpu.VMEM((mh, S5_ST), f32),
            pltpu.VMEM((bsz, S5_ST), f32),
            pltpu.VMEM((bsz, S5_ST), f32),
        ],
        compiler_params=_cparams(3),
        name="s5_core",
    )(u4.reshape(rows, MIX), bb, cc, prow, pcol)


def _s5_post_kernel(yf_ref, yb_ref, u_ref, d_ref, w_ref, b_ref, o_ref):
    n = o_ref.shape[0]
    y = (yf_ref[...] + yb_ref[...] + d_ref[...] * u_ref[...]).reshape(n, MIX)
    y = 0.5 * y * (1.0 + jnp.tanh(math.sqrt(2.0 / math.pi) * (y + 0.044715 * (y * y * y))))
    o_ref[...] = (y * _sigmoid(_mm(y, w_ref[...]) + b_ref[...])).astype(o_ref.dtype)


def _s5_post(y2, u4, d_skip, glu_w, glu_b):
    n_chunks, bsz, c, _ = u4.shape
    seq = n_chunks * c
    tok = min(S5_TOK, seq)
    nck = tok // c
    y5 = y2.reshape(2, n_chunks, bsz, c, MIX)
    row = pl.BlockSpec((1, MIX), lambda b, i: (0, 0))
    return pl.pallas_call(
        _s5_post_kernel,
        out_shape=jax.ShapeDtypeStruct((bsz * seq, MIX), bf16),
        grid=(bsz, seq // tok),
        in_specs=[
            pl.BlockSpec((None, nck, None, c, MIX), lambda b, i: (0, i, b, 0, 0)),
            pl.BlockSpec((None, nck, None, c, MIX), lambda b, i: (1, i, b, 0, 0)),
            pl.BlockSpec((nck, None, c, MIX), lambda b, i: (i, b, 0, 0)),
            row,
            pl.BlockSpec((MIX, MIX), lambda b, i: (0, 0)),
            row,
        ],
        out_specs=pl.BlockSpec((tok, MIX), lambda b, i: (b * (seq // tok) + i, 0)),
        compiler_params=_cparams(2),
        name="s5_post",
    )(y5, y5, u4, d_skip, glu_w, glu_b)


def _s5_tables(a_re, a_im, log_dt, b_re, b_im, c_re, c_im):
    c = S5_CHUNK
    dt = jnp.exp(log_dt)[..., None]
    mag = a_re * dt
    ph = a_im * dt
    steps = jnp.arange(c + 1, dtype=f32)[:, None, None, None]
    m = jnp.exp(mag[None] * steps)
    pr = m * jnp.cos(ph[None] * steps)
    pi = m * jnp.sin(ph[None] * steps)
    den = a_re * a_re + a_im * a_im
    qr = ((pr[1] - 1.0) * a_re + pi[1] * a_im) / den
    qi = (pi[1] * a_re - (pr[1] - 1.0) * a_im) / den
    bbr = (qr[..., None] * b_re - qi[..., None] * b_im).transpose(0, 1, 3, 2)
    bbi = (qr[..., None] * b_im + qi[..., None] * b_re).transpose(0, 1, 3, 2)
    bb = jnp.stack([bbr, bbi], axis=1).reshape(2, 2, S5_NSG, S5_SG * S5_CG, S5_P).transpose(0, 2, 1, 3, 4)
    cc = jnp.stack([c_re, c_im], axis=0).transpose(0, 1, 3, 2).reshape(2, S5_NSG, S5_ST, S5_CG)
    cc = jnp.broadcast_to(cc.transpose(1, 0, 2, 3)[None], (2, S5_NSG, 2, S5_ST, S5_CG))
    d_idx = jnp.arange(2)[:, None]
    steps_i = jnp.arange(c)
    fsel = jnp.stack([c - 1 - steps_i, steps_i], axis=0)
    fsel = jnp.concatenate([fsel, jnp.full((2, 1), c)], axis=1)
    tab = lambda p, sel: p[sel, d_idx].reshape(2, sel.shape[1], S5_NSG, S5_ST)
    prow = jnp.stack([tab(pr, fsel), tab(pi, fsel)], axis=1).transpose(0, 3, 1, 2, 4)
    prow = jnp.concatenate([prow, jnp.zeros((2, S5_NSG, 2, 24 - c - 1, S5_ST), f32)], axis=3)
    esel = jnp.stack([steps_i + 1, c - steps_i], axis=0)
    esel = jnp.concatenate([esel, jnp.broadcast_to(steps_i, (2, c))], axis=1)
    pcol = jnp.stack([tab(pr, esel), tab(pi, esel)], axis=1).transpose(0, 3, 1, 4, 2)
    return bb, cc, prow, pcol


def _block_diag_ones(block, scale=1.0):
    idx = jnp.arange(MIX) // block
    return ((idx[:, None] == idx[None, :]).astype(f32) * scale).astype(bf16)


def kernel(x, p, ffn1_norm, ffn1_w1, ffn1_w3, ffn1_w2, mix_norm, w_in, rwkv_mu, rwkv_w0, rwkv_w2, rwkv_a0, rwkv_a2, rwkv_g2, rwkv_k_k, rwkv_k_a, rwkv_r_k, rwkv_ln_w, rwkv_ln_b, dn_conv, dn_a_log, dn_dt_bias, dn_norm, s5_a_re, s5_a_im, s5_log_dt, s5_b_re, s5_b_im, s5_c_re, s5_c_im, s5_d, s5_glu_w, s5_glu_b, w_branch_a, w_branch_b, w_branch_c, w_out, ffn2_norm, ffn2_w1, ffn2_w3, ffn2_w2, ple_norm, ple_w_gate, ple_w_proj, final_norm):
    bsz, seq, _ = x.shape
    t = bsz * seq
    depth = p.shape[0]
    row = lambda a: a.reshape(1, -1).astype(f32)
    ones64 = _block_diag_ones(RW_HD)
    avg64 = _block_diag_ones(RW_HD, 1.0 / RW_HD)
    ones128 = _block_diag_ones(DN_HD)
    avg128 = _block_diag_ones(DN_HD, 1.0 / DN_HD)
    o_rw, o_dn, o_s5, o_gate = 0, 3488, 3488 + 4128, 3488 + 4128 + 1024

    xt = x.reshape(t, D_MODEL)
    for i in range(depth):
        xt = _ffn(xt, row(ffn1_norm[i]), ffn1_w1[i].astype(bf16), ffn1_w3[i].astype(bf16), ffn1_w2[i].astype(bf16))

        wi = w_in[i]
        g_mix = row(mix_norm[i])
        w_all = jnp.concatenate([
            wi[:, o_rw : o_rw + 3488], jnp.zeros((D_MODEL, PROJ_DNS - 3488), f32),
            wi[:, o_dn + 4 * MIX : o_dn + 4128], jnp.zeros((D_MODEL, PROJ_DN - PROJ_DNS - 4 * DN_HEADS), f32),
            wi[:, o_dn : o_dn + 4 * MIX]], axis=1).astype(bf16)
        w_s5 = wi[:, o_s5 : o_s5 + MIX].astype(bf16)
        w_gate = wi[:, o_gate:].astype(bf16)
        c_all = _norm_mm(xt, g_mix, w_all).reshape(bsz, seq, PROJ_COLS)

        mu = jnp.concatenate([rwkv_mu[i], jnp.zeros((RW_COLS - 3488,), f32)]).reshape(1, RW_COLS)
        zeros_blk = jnp.zeros((64, MIX), f32)
        w2cat = jnp.concatenate([
            jnp.concatenate([rwkv_w2[i, 0], zeros_blk], axis=1),
            jnp.concatenate([zeros_blk, rwkv_w2[i, 1]], axis=1)], axis=0).astype(bf16)
        a2cat = jnp.concatenate([
            jnp.concatenate([rwkv_a2[i, 0], zeros_blk], axis=1),
            jnp.concatenate([zeros_blk, rwkv_a2[i, 1]], axis=1)], axis=0).astype(bf16)
        g2pad = jnp.concatenate([rwkv_g2[i], jnp.zeros((256 - 160, MIX), f32)], axis=0).astype(bf16)
        prep = _rwkv_prep(c_all, mu, row(rwkv_w0[i]), row(rwkv_a0[i]), w2cat, a2cat, g2pad,
                          row(rwkv_k_k[i]), row(rwkv_k_a[i]), row(rwkv_r_k[i]), ones64)
        r_, v_, an_, lwf_, lwb_, kf_, kb_, bf_, bb_, gate_, bonus_ = prep
        yf_rw, yb_rw = _rwkv_scan(r_, v_, an_, lwf_, lwb_, kf_, kb_, bf_, bb_)
        ya = _rwkv_post(yf_rw.reshape(t, MIX), yb_rw.reshape(t, MIX), bonus_.reshape(t, MIX), gate_.reshape(t, MIX),
                        row(rwkv_ln_w[i]), row(rwkv_ln_b[i]), avg64)

        conv8 = jnp.concatenate([dn_conv[i], jnp.zeros((8 - DN_CONV, 3 * MIX), f32)], axis=0)
        q_, k_, v2_ = _dn_prep(c_all, conv8, ones128)
        pad16 = lambda a: jnp.concatenate([jnp.zeros((2 * DN_HEADS,), f32), a.reshape(-1), jnp.zeros((128 - 4 * DN_HEADS,), f32)]).reshape(1, 128)
        of_dn, ob_dn = _dn_scan(q_, k_, v2_, c_all, pad16(dn_a_log[i]), pad16(dn_dt_bias[i]))
        yb = _dn_post(of_dn.reshape(t, MIX), ob_dn.reshape(t, MIX), c_all.reshape(t, PROJ_COLS), jnp.tile(dn_norm[i], DN_HEADS).reshape(1, MIX), avg128)

        u4 = _s5_proj(xt.reshape(bsz, seq, D_MODEL), g_mix, w_s5)
        ops5 = _s5_tables(s5_a_re[i], s5_a_im[i], s5_log_dt[i], s5_b_re[i], s5_b_im[i], s5_c_re[i], s5_c_im[i])
        yc = _s5_post(_s5_core(u4, *ops5), u4, row(s5_d[i]), s5_glu_w[i].astype(bf16), row(s5_glu_b[i]))

        xt = _merge(xt, g_mix, ya, yb, yc, w_gate, w_branch_a[i].astype(bf16), w_branch_b[i].astype(bf16),
                    w_branch_c[i].astype(bf16), w_out[i].astype(bf16))
        xt = _ffn(xt, row(ffn2_norm[i]), ffn2_w1[i].astype(bf16), ffn2_w3[i].astype(bf16), ffn2_w2[i].astype(bf16))
        xt = _ple(xt, p[i].reshape(t, PLE_DIM), row(ple_norm[i]), ple_w_gate[i].astype(bf16), ple_w_proj[i].astype(bf16))
    return _final_norm(xt, row(final_norm)).reshape(bsz, seq, D_MODEL)
```

```python
import functools
import math

import jax
import jax.numpy as jnp
from jax import lax
from jax.experimental import pallas as pl
from jax.experimental.pallas import tpu as pltpu

f32 = jnp.float32
bf16 = jnp.bfloat16

D_MODEL = 2048
D_FF = 5632
PLE_DIM = 256
NORM_EPS = 1e-6
MIX = 1024

RW_HEADS = 16
RW_HD = 64
RW_DECAY_SCALE = math.exp(-0.5)
RW_GN_EPS = 64e-5
RW_COLS = 3584
RW_CHUNK = 64

DN_HEADS = 8
DN_HD = 128
DN_CONV = 5
DN_CHUNK = 64

S5_G = 64
S5_CG = 16
S5_P = 64
S5_CHUNK = 16

PROJ_RW = 0
PROJ_DNS = 3584
PROJ_DN = 4096
PROJ_COLS = 8192

VMEM_LIMIT = 56 * 1024 * 1024


def _cparams(n_axes):
    return pltpu.CompilerParams(
        dimension_semantics=("arbitrary",) * n_axes, vmem_limit_bytes=VMEM_LIMIT
    )


def _mm(a, b):
    return jnp.dot(a.astype(bf16), b.astype(bf16), preferred_element_type=f32)


def _bmm(a, b):
    return lax.dot_general(a.astype(bf16), b.astype(bf16), (((2,), (1,)), ((0,), (0,))), preferred_element_type=f32)


def _bmm_nt(a, b):
    return lax.dot_general(a.astype(bf16), b.astype(bf16), (((2,), (2,)), ((0,), (0,))), preferred_element_type=f32)


def _bmm_tn(a, b):
    return lax.dot_general(a.astype(bf16), b.astype(bf16), (((1,), (1,)), ((0,), (0,))), preferred_element_type=f32)


def _mm2(a, b01):
    hi = a.astype(bf16)
    lo = (a - hi.astype(f32)).astype(bf16)
    return jnp.dot(hi, b01, preferred_element_type=f32) + jnp.dot(lo, b01, preferred_element_type=f32)


def _mm3(m01, x):
    hi = x.astype(bf16)
    r1 = x - hi.astype(f32)
    mid = r1.astype(bf16)
    lo = (r1 - mid.astype(f32)).astype(bf16)
    dot = lambda p: jnp.dot(m01, p, preferred_element_type=f32)
    return dot(hi) + dot(mid) + dot(lo)


def _rms(x, g):
    return x * lax.rsqrt(jnp.mean(x * x, axis=-1, keepdims=True) + NORM_EPS) * g


def _sigmoid(x):
    return 1.0 / (1.0 + jnp.exp(-x))


def _silu(x):
    return x * _sigmoid(x)


def _softplus(x):
    return jnp.maximum(x, 0.0) + jnp.log(1.0 + jnp.exp(-jnp.abs(x)))


def _ffn_kernel(x_ref, g_ref, w1_ref, w3_ref, w2_ref, o_ref, h_ref):
    @pl.when(pl.program_id(1) == 0)
    def _():
        x = x_ref[...]
        h_ref[...] = _rms(x, g_ref[...]).astype(bf16)
        o_ref[...] = x

    h = h_ref[...]
    a = jnp.dot(h, w1_ref[...], preferred_element_type=f32)
    b = jnp.dot(h, w3_ref[...], preferred_element_type=f32)
    act = (0.5 * _silu(a) * b).astype(bf16)
    o_ref[...] += jnp.dot(act, w2_ref[...], preferred_element_type=f32)


def _ffn(x, g, w1, w3, w2, tm=512, tf=512):
    t = x.shape[0]
    tm = min(tm, t)
    return pl.pallas_call(
        _ffn_kernel,
        out_shape=jax.ShapeDtypeStruct((t, D_MODEL), f32),
        grid=(t // tm, D_FF // tf),
        in_specs=[
            pl.BlockSpec((tm, D_MODEL), lambda i, j: (i, 0)),
            pl.BlockSpec((1, D_MODEL), lambda i, j: (0, 0)),
            pl.BlockSpec((D_MODEL, tf), lambda i, j: (0, j)),
            pl.BlockSpec((D_MODEL, tf), lambda i, j: (0, j)),
            pl.BlockSpec((tf, D_MODEL), lambda i, j: (j, 0)),
        ],
        out_specs=pl.BlockSpec((tm, D_MODEL), lambda i, j: (i, 0)),
        scratch_shapes=[pltpu.VMEM((tm, D_MODEL), bf16)],
        compiler_params=_cparams(2),
        name="ffn",
    )(x, g, w1, w3, w2)


def _norm_mm_kernel(x_ref, g_ref, w_ref, o_ref, h_ref):
    @pl.when(pl.program_id(1) == 0)
    def _():
        h_ref[...] = _rms(x_ref[...], g_ref[...]).astype(bf16)

    o_ref[...] = jnp.dot(h_ref[...], w_ref[...], preferred_element_type=f32)


def _norm_mm(x, g, w, tm=1024, tn=1024):
    t = x.shape[0]
    n = w.shape[1]
    tm = min(tm, t)
    return pl.pallas_call(
        _norm_mm_kernel,
        out_shape=jax.ShapeDtypeStruct((t, n), f32),
        grid=(t // tm, n // tn),
        in_specs=[
            pl.BlockSpec((tm, D_MODEL), lambda i, j: (i, 0)),
            pl.BlockSpec((1, D_MODEL), lambda i, j: (0, 0)),
            pl.BlockSpec((D_MODEL, tn), lambda i, j: (0, j)),
        ],
        out_specs=pl.BlockSpec((tm, tn), lambda i, j: (i, j)),
        scratch_shapes=[pltpu.VMEM((tm, D_MODEL), bf16)],
        compiler_params=_cparams(2),
        name="norm_mm",
    )(x, g, w)


def _merge_kernel(x_ref, g_ref, ya_ref, yb_ref, yc_ref, wga_ref, wgb_ref, wgc_ref, wa_ref, wb_ref, wc_ref, wo_ref, o_ref, h_ref):
    @pl.when(pl.program_id(1) == 0)
    def _():
        x = x_ref[...]
        h_ref[...] = _rms(x, g_ref[...]).astype(bf16)
        o_ref[...] = x

    h = h_ref[...]
    m = None
    for y_ref, wg_ref, w_ref in ((ya_ref, wga_ref, wa_ref), (yb_ref, wgb_ref, wb_ref), (yc_ref, wgc_ref, wc_ref)):
        gate = _sigmoid(jnp.dot(h, wg_ref[...], preferred_element_type=f32))
        term = gate * jnp.dot(y_ref[...], w_ref[...], preferred_element_type=f32)
        m = term if m is None else m + term
    o_ref[...] += jnp.dot(m.astype(bf16), wo_ref[...], preferred_element_type=f32)


def _merge(x, g, ya, yb, yc, w_gate, wa, wb, wc, wo, tm=512, tn=512):
    t = x.shape[0]
    tm = min(tm, t)
    nj = D_MODEL // tn
    tok = pl.BlockSpec((tm, MIX), lambda i, j: (i, 0))
    gate_w = lambda b: pl.BlockSpec((D_MODEL, tn), lambda i, j: (0, b * nj + j))
    br_w = pl.BlockSpec((MIX, tn), lambda i, j: (0, j))
    return pl.pallas_call(
        _merge_kernel,
        out_shape=jax.ShapeDtypeStruct((t, D_MODEL), f32),
        grid=(t // tm, nj),
        in_specs=[
            pl.BlockSpec((tm, D_MODEL), lambda i, j: (i, 0)),
            pl.BlockSpec((1, D_MODEL), lambda i, j: (0, 0)),
            tok, tok, tok,
            gate_w(0), gate_w(1), gate_w(2),
            br_w, br_w, br_w,
            pl.BlockSpec((tn, D_MODEL), lambda i, j: (j, 0)),
        ],
        out_specs=pl.BlockSpec((tm, D_MODEL), lambda i, j: (i, 0)),
        scratch_shapes=[pltpu.VMEM((tm, D_MODEL), bf16)],
        compiler_params=_cparams(2),
        name="merge",
    )(x, g, ya, yb, yc, w_gate, w_gate, w_gate, wa, wb, wc, wo)


def _ple_kernel(x_ref, xj_ref, p_ref, g_ref, wg_ref, wp_ref, o_ref, h_ref):
    @pl.when(pl.program_id(1) == 0)
    def _():
        h_ref[...] = _rms(x_ref[...], g_ref[...]).astype(bf16)

    gate = _sigmoid(jnp.dot(h_ref[...], wg_ref[...], preferred_element_type=f32))
    emb = jnp.dot(p_ref[...].astype(bf16), wp_ref[...], preferred_element_type=f32)
    o_ref[...] = xj_ref[...] + emb * gate


def _ple(x, p, g, wg, wp, tm=512, tn=1024):
    t = x.shape[0]
    tm = min(tm, t)
    return pl.pallas_call(
        _ple_kernel,
        out_shape=jax.ShapeDtypeStruct((t, D_MODEL), f32),
        grid=(t // tm, D_MODEL // tn),
        in_specs=[
            pl.BlockSpec((tm, D_MODEL), lambda i, j: (i, 0)),
            pl.BlockSpec((tm, tn), lambda i, j: (i, j)),
            pl.BlockSpec((tm, PLE_DIM), lambda i, j: (i, 0)),
            pl.BlockSpec((1, D_MODEL), lambda i, j: (0, 0)),
            pl.BlockSpec((D_MODEL, tn), lambda i, j: (0, j)),
            pl.BlockSpec((PLE_DIM, tn), lambda i, j: (0, j)),
        ],
        out_specs=pl.BlockSpec((tm, tn), lambda i, j: (i, j)),
        scratch_shapes=[pltpu.VMEM((tm, D_MODEL), bf16)],
        compiler_params=_cparams(2),
        name="ple",
    )(x, x, p, g, wg, wp)


def _final_norm_kernel(x_ref, g_ref, o_ref):
    o_ref[...] = _rms(x_ref[...], g_ref[...])


def _final_norm(x, g, tm=512):
    t = x.shape[0]
    tm = min(tm, t)
    return pl.pallas_call(
        _final_norm_kernel,
        out_shape=jax.ShapeDtypeStruct((t, D_MODEL), f32),
        grid=(t // tm,),
        in_specs=[
            pl.BlockSpec((tm, D_MODEL), lambda i: (i, 0)),
            pl.BlockSpec((1, D_MODEL), lambda i: (0, 0)),
        ],
        out_specs=pl.BlockSpec((tm, D_MODEL), lambda i: (i, 0)),
        compiler_params=_cparams(1),
        name="final_norm",
    )(x, g)


def _halo(prev_ref, c_ref, next_ref, n_tiles):
    i = pl.program_id(1)
    prev = jnp.where(i == 0, 0.0, prev_ref[...])
    nxt = jnp.where(i == n_tiles - 1, 0.0, next_ref[...])
    return jnp.concatenate([prev, c_ref[...], nxt], axis=0)


def _shifted(ext, k, tl):
    n = ext.shape[0]
    return pltpu.roll(ext, (-k) % n, axis=0)[8 : 8 + tl]


def _iota2(c):
    return lax.broadcasted_iota(jnp.int32, (c, c), 0), lax.broadcasted_iota(jnp.int32, (c, c), 1)


def _by_dir(x, fn_f, fn_b):
    n = x.shape[0] // 2
    return jnp.concatenate([fn_f(x[:n]), fn_b(x[n:])], axis=0)


def _mask_dir(x, mask_f, mask_b):
    return _by_dir(x, lambda a: jnp.where(mask_f, a, 0.0), lambda a: jnp.where(mask_b, a, 0.0))


def _unit_tri_inverse(m, c):
    ii, jj = _iota2(c)
    same = lambda sh: (ii >> sh) == (jj >> sh)
    d = (ii == jj).astype(f32) + _mask_dir(m, same(1) & (ii > jj), same(1) & (jj > ii))
    sh = 1
    while (1 << sh) < c:
        hi_i = ((ii >> sh) & 1) == 1
        hi_j = ((jj >> sh) & 1) == 1
        ms = _mask_dir(m, same(sh + 1) & hi_i & ~hi_j, same(sh + 1) & hi_j & ~hi_i)
        d = d + _bmm(d, _bmm(ms, d))
        sh += 1
    return d


def _rwkv_prep_kernel(
    prev_ref, c_ref, next_ref, mu_ref, w0_ref, a0_ref, w2_ref, a2_ref, g2_ref, kk_ref, ka_ref, rk_ref, ones_ref,
    r_o, v_o, an_o, lwf_o, lwb_o, kf_o, kb_o, bf_o, bb_o, gate_o, bonus_o, *, tl, n_tiles
):
    ext = _halo(prev_ref, c_ref, next_ref, n_tiles)
    c = ext[8 : 8 + tl]
    cs = c + (0.5 * (_shifted(ext, -1, tl) + _shifted(ext, 1, tl)) - c) * mu_ref[...]
    r = cs[:, 0:MIX]
    k = cs[:, MIX : 2 * MIX]
    v = cs[:, 2 * MIX : 3 * MIX]
    lw_in = jnp.tanh(cs[:, 3 * MIX : 3 * MIX + 128])
    la_in = cs[:, 3 * MIX + 128 : 3 * MIX + 256]
    lg_in = _sigmoid(cs[:, 3 * MIX + 256 : 3 * MIX + 512])
    logw = -RW_DECAY_SCALE * _sigmoid(w0_ref[...] + _mm(lw_in, w2_ref[...]))
    a = _sigmoid(a0_ref[...] + _mm(la_in, a2_ref[...]))
    gate_o[...] = _mm(lg_in, g2_ref[...]).astype(gate_o.dtype)
    ones = ones_ref[...]
    kx = k * kk_ref[...]
    kk = kx * lax.rsqrt(_mm2(kx * kx, ones) + 1e-6)
    a_f = a[:, 0:MIX]
    a_b = a[:, MIX : 2 * MIX]
    ka = ka_ref[...]
    k_f = k * (1.0 + (a_f - 1.0) * ka)
    k_b = k * (1.0 + (a_b - 1.0) * ka)
    r_o[...] = r.astype(r_o.dtype)
    v_o[...] = v.astype(v_o.dtype)
    an_o[...] = (-kk).astype(an_o.dtype)
    lwf_o[...] = logw[:, 0:MIX]
    lwb_o[...] = logw[:, MIX : 2 * MIX]
    kf_o[...] = k_f.astype(kf_o.dtype)
    kb_o[...] = k_b.astype(kb_o.dtype)
    bf_o[...] = (kk * a_f).astype(bf_o.dtype)
    bb_o[...] = (kk * a_b).astype(bb_o.dtype)
    bonus_o[...] = (_mm2(r * (k_f + k_b) * rk_ref[...], ones) * v).astype(bonus_o.dtype)


def _rwkv_prep(c, mu, w0, a0, w2, a2, g2, k_k, k_a, r_k, ones64, tl=256):
    bsz, seq, _ = c.shape
    tl = min(tl, seq)
    n_tiles = seq // tl
    nb = tl // 8
    row = lambda a: pl.BlockSpec(a.shape, lambda b, i: (0,) * a.ndim)
    outs = [jax.ShapeDtypeStruct((bsz, seq, MIX), f32 if n in (3, 4) else bf16) for n in range(11)]
    ospec = pl.BlockSpec((None, tl, MIX), lambda b, i: (b, i, 0))
    return pl.pallas_call(
        functools.partial(_rwkv_prep_kernel, tl=tl, n_tiles=n_tiles),
        out_shape=outs,
        grid=(bsz, n_tiles),
        in_specs=[
            pl.BlockSpec((None, 8, RW_COLS), lambda b, i: (b, jnp.maximum(i * nb - 1, 0), 0)),
            pl.BlockSpec((None, tl, RW_COLS), lambda b, i: (b, i, 0)),
            pl.BlockSpec((None, 8, RW_COLS), lambda b, i: (b, jnp.minimum((i + 1) * nb, seq // 8 - 1), 0)),
            row(mu), row(w0), row(a0), row(w2), row(a2), row(g2), row(k_k), row(k_a), row(r_k), row(ones64),
        ],
        out_specs=[ospec] * 11,
        compiler_params=_cparams(2),
        name="rwkv_prep",
    )(c, c, c, mu, w0, a0, w2, a2, g2, k_k, k_a, r_k, ones64)


def _pair_diag(x):
    lo = lax.broadcasted_iota(jnp.int32, x.shape, 2) < RW_HD
    zero = jnp.zeros_like(x)
    return jnp.concatenate([jnp.where(lo, x, zero), jnp.where(lo, zero, x)], axis=1)


def _pair_tri_inverse(m, ii, jj):
    same = lambda sh: (ii >> sh) == (jj >> sh)
    d = (ii == jj).astype(f32) + _mask_dir(m, same(1) & (ii > jj), same(1) & (jj > ii))
    sh = 1
    while (1 << sh) < RW_CHUNK:
        hi_i = ((ii >> sh) & 1) == 1
        hi_j = ((jj >> sh) & 1) == 1
        ms = _mask_dir(m, same(sh + 1) & hi_i & ~hi_j, same(sh + 1) & hi_j & ~hi_i)
        inner = _bmm(ms, _pair_diag(d.astype(bf16)))
        d = d + _bmm(d, _pair_diag(inner.astype(bf16)))
        sh += 1
    return d


def _rwkv_chunks(s, at, rt, bt, kt, v, bh, kh, e_tot):
    c = RW_CHUNK
    ii = lax.broadcasted_iota(jnp.int32, (c, 2 * c), 0)
    jj = lax.broadcasted_iota(jnp.int32, (c, 2 * c), 1) & (c - 1)
    ar = jnp.concatenate([at, rt], axis=1)
    rb = _bmm_nt(ar, _pair_diag(bt))
    rk = _bmm_nt(ar, _pair_diag(kt))
    a_ab = _mask_dir(rb[:, :c], jj < ii, jj > ii)
    a_ak = _mask_dir(rk[:, :c], jj < ii, jj > ii)
    p_b = _mask_dir(rb[:, c:], jj <= ii, jj >= ii)
    p_k = _mask_dir(rk[:, c:], jj <= ii, jj >= ii)
    tinv = _pair_tri_inverse(a_ab, ii, jj)
    v_d = _pair_diag(v)
    akv = _bmm(a_ak, v_d)
    wu = _bmm(tinv, jnp.concatenate([_pair_diag(at), _pair_diag(akv.astype(bf16))], axis=2))
    w = wu[:, :, :128]
    ws = _bmm_nt(jnp.concatenate([w.astype(bf16), rt], axis=1), s)
    u = (ws[:, :c] + wu[:, :, 128:]).astype(bf16)
    y = ws[:, c:] + _bmm(jnp.concatenate([p_b, p_k], axis=2), jnp.concatenate([_pair_diag(u), v_d], axis=1))
    upd = _bmm_tn(jnp.concatenate([u, v], axis=1), jnp.concatenate([bh, kh], axis=1))
    r2 = lax.broadcasted_iota(jnp.int32, (2 * c, 2 * c), 0) >> 6
    c2 = lax.broadcasted_iota(jnp.int32, (2 * c, 2 * c), 1) >> 6
    s_new = s * e_tot + jnp.where(r2 == c2, upd, 0.0)
    return y, s_new


def _rwkv_scan_kernel(rf, vf, anf, lwf, kf, bf_, rb, vb, anb, lwb, kb, bb, yf_ref, yb_ref, s_ref):
    c = RW_CHUNK

    @pl.when(pl.program_id(1) == 0)
    def _():
        s_ref[...] = jnp.zeros_like(s_ref)

    ii, jj = _iota2(c)

    def side(r_ref, v_ref, an_ref, lw_ref, k_ref, b_ref, cum, last):
        lw = lw_ref[...]
        cl = _mm3(cum.astype(bf16), lw)
        e_neg = jnp.exp(-cl)
        cl_last = cl[last : last + 1, :]
        e_last = jnp.exp(cl_last - cl)
        b = b_ref[...].astype(f32)
        kd = k_ref[...].astype(f32)
        parts = (an_ref[...].astype(f32) * jnp.exp(cl - lw), r_ref[...].astype(f32) * jnp.exp(cl), b * e_neg, kd * e_neg, v_ref[...], b * e_last, kd * e_last)
        return [p.astype(bf16) for p in parts], jnp.exp(cl_last)

    pf, ef = side(rf, vf, anf, lwf, kf, bf_, jj <= ii, c - 1)
    pb, eb = side(rb, vb, anb, lwb, kb, bb, jj >= ii, 0)
    n_pairs = RW_HEADS // 2
    pairs = lambda x: [x[:, p * 128 : (p + 1) * 128] for p in range(n_pairs)]
    ops = [jnp.stack(pairs(a) + pairs(b), axis=0) for a, b in zip(pf, pb)]
    e_tot = jnp.stack(pairs(ef) + pairs(eb), axis=0)
    y, s_new = _rwkv_chunks(s_ref[...], *ops, e_tot)
    s_ref[...] = s_new
    for p in range(n_pairs):
        yf_ref[:, p * 128 : (p + 1) * 128] = y[p].astype(yf_ref.dtype)
        yb_ref[:, p * 128 : (p + 1) * 128] = y[n_pairs + p].astype(yb_ref.dtype)


def _rwkv_scan(r, v, an, lwf, lwb, kf, kb, bf_, bb):
    bsz, seq, _ = r.shape
    n_chunks = seq // RW_CHUNK
    fwd = pl.BlockSpec((None, RW_CHUNK, MIX), lambda b, i: (b, i, 0))
    bwd = pl.BlockSpec((None, RW_CHUNK, MIX), lambda b, i: (b, n_chunks - 1 - i, 0))
    out = jax.ShapeDtypeStruct((bsz, seq, MIX), bf16)
    return pl.pallas_call(
        _rwkv_scan_kernel,
        out_shape=[out, out],
        grid=(bsz, n_chunks),
        in_specs=[fwd] * 6 + [bwd] * 6,
        out_specs=[fwd, bwd],
        scratch_shapes=[pltpu.VMEM((RW_HEADS, 2 * RW_HD, 2 * RW_HD), f32)],
        compiler_params=_cparams(2),
        name="rwkv_scan",
    )(r, v, an, lwf, kf, bf_, r, v, an, lwb, kb, bb)


def _rwkv_post_kernel(yf_ref, yb_ref, bonus_ref, gate_ref, lnw_ref, lnb_ref, avg_ref, o_ref):
    y = yf_ref[...].astype(f32) + yb_ref[...].astype(f32)
    avg = avg_ref[...]
    mean = _mm2(y, avg)
    yc = y - mean
    var = _mm2(yc * yc, avg)
    yn = yc * lax.rsqrt(var + RW_GN_EPS) * lnw_ref[...] + lnb_ref[...]
    o_ref[...] = ((yn + bonus_ref[...].astype(f32)) * gate_ref[...].astype(f32)).astype(o_ref.dtype)


def _rwkv_post(yf, yb, bonus, gate, ln_w, ln_b, avg64, tm=512):
    t = yf.shape[0]
    tm = min(tm, t)
    tok = pl.BlockSpec((tm, MIX), lambda i: (i, 0))
    row = pl.BlockSpec((1, MIX), lambda i: (0, 0))
    return pl.pallas_call(
        _rwkv_post_kernel,
        out_shape=jax.ShapeDtypeStruct((t, MIX), bf16),
        grid=(t // tm,),
        in_specs=[tok, tok, tok, tok, row, row, pl.BlockSpec((MIX, MIX), lambda i: (0, 0))],
        out_specs=tok,
        compiler_params=_cparams(1),
        name="rwkv_post",
    )(yf, yb, bonus, gate, ln_w, ln_b, avg64)


def _dn_prep_kernel(*refs, tl, n_tiles):
    halos, (cw_ref, ones_ref, q_o, k_o, v_o) = refs[:9], refs[9:]
    cw = cw_ref[...]
    parts = []
    for p in range(3):
        ext = _halo(*halos[3 * p : 3 * p + 3], n_tiles)
        acc = None
        for j in range(DN_CONV):
            term = _shifted(ext, j - DN_CONV // 2, tl) * cw[j : j + 1, p * MIX : (p + 1) * MIX]
            acc = term if acc is None else acc + term
        parts.append(_silu(acc))
    q, k, v = parts
    ones = ones_ref[...]
    q_o[...] = (q * lax.rsqrt(_mm2(q * q, ones) + 1e-6) * (DN_HD ** -0.5)).astype(q_o.dtype)
    k_o[...] = (k * lax.rsqrt(_mm2(k * k, ones) + 1e-6)).astype(k_o.dtype)
    v_o[...] = v.astype(v_o.dtype)


def _dn_prep(c, conv_w8, ones128, tl=256):
    bsz, seq, _ = c.shape
    tl = min(tl, seq)
    n_tiles = seq // tl
    nb = tl // 8
    out = jax.ShapeDtypeStruct((bsz, seq, MIX), bf16)
    ospec = pl.BlockSpec((None, tl, MIX), lambda b, i: (b, i, 0))
    halo_specs = []
    for p in range(3):
        col = PROJ_DN // MIX + p
        halo_specs += [
            pl.BlockSpec((None, 8, MIX), lambda b, i, col=col: (b, jnp.maximum(i * nb - 1, 0), col)),
            pl.BlockSpec((None, tl, MIX), lambda b, i, col=col: (b, i, col)),
            pl.BlockSpec((None, 8, MIX), lambda b, i, col=col: (b, jnp.minimum((i + 1) * nb, seq // 8 - 1), col)),
        ]
    return pl.pallas_call(
        functools.partial(_dn_prep_kernel, tl=tl, n_tiles=n_tiles),
        out_shape=[out] * 3,
        grid=(bsz, n_tiles),
        in_specs=halo_specs + [
            pl.BlockSpec((8, 3 * MIX), lambda b, i: (0, 0)),
            pl.BlockSpec((MIX, MIX), lambda b, i: (0, 0)),
        ],
        out_specs=[ospec] * 3,
        compiler_params=_cparams(2),
        name="dn_prep",
    )(*([c] * 9), conv_w8, ones128)


def _dn_chunks(s, q, k, v, beta, gcol, grow):
    c = DN_CHUNK
    ii, jj = _iota2(c)
    incl_f, incl_b = jj <= ii, jj >= ii
    diff = gcol[:, :, :c] - grow
    decay = _by_dir(
        diff,
        lambda a: jnp.where(incl_f, jnp.exp(jnp.where(incl_f, a, 0.0)), 0.0),
        lambda a: jnp.where(incl_b, jnp.exp(jnp.where(incl_b, a, 0.0)), 0.0),
    )
    kb = k * beta
    r = _bmm_nt(jnp.concatenate([kb, q], axis=1), k)
    nmat = _mask_dir(r[:, :c] * decay, jj < ii, jj > ii)
    qk = r[:, c:] * decay
    tinv = _unit_tri_inverse(-nmat, c)
    e_gc = jnp.exp(gcol)
    uw = _bmm(tinv, jnp.concatenate([v * beta, kb * e_gc], axis=2))
    u = uw[:, :, :DN_HD]
    w = uw[:, :, DN_HD:]
    wq = _bmm(jnp.concatenate([w, q * e_gc], axis=1), s)
    v_new = u - wq[:, :c]
    o = wq[:, c:] + _bmm(qk, v_new)
    gl = _by_dir(gcol, lambda a: a[:, c - 1 : c, :], lambda a: a[:, 0:1, :])
    s_new = s * jnp.exp(gl) + _bmm_tn(k * jnp.exp(gl - gcol), v_new)
    return o, s_new


def _dn_scan_kernel(qf, kf, vf, gf, qb, kb, vb, gb, alog_ref, dtb_ref, of_ref, ob_ref, s_ref):
    c = DN_CHUNK

    @pl.when(pl.program_id(1) == 0)
    def _():
        s_ref[...] = jnp.zeros_like(s_ref)

    ii, jj = _iota2(c)
    qs, ks, vs, betas, gcols, grows = [], [], [], [], [], []
    for d, (q_ref, k_ref, v_ref, g_ref, cum) in enumerate(((qf, kf, vf, gf, jj <= ii), (qb, kb, vb, gb, jj >= ii))):
        x = g_ref[...]
        beta = _sigmoid(x)
        g = -jnp.exp(alog_ref[...]) * _softplus(x + dtb_ref[...])
        gc = _mm3(cum.astype(bf16), g)
        gct = gc.T
        q, k, v = q_ref[...].astype(f32), k_ref[...].astype(f32), v_ref[...].astype(f32)
        for h in range(DN_HEADS):
            col = 2 * DN_HEADS + DN_HEADS * d + h
            bcol = DN_HEADS * d + h
            sl = slice(h * DN_HD, (h + 1) * DN_HD)
            qs.append(q[:, sl])
            ks.append(k[:, sl])
            vs.append(v[:, sl])
            betas.append(jnp.broadcast_to(beta[:, bcol : bcol + 1], (c, DN_HD)))
            gcols.append(jnp.broadcast_to(gc[:, col : col + 1], (c, DN_HD)))
            grows.append(gct[col : col + 1, :])
    st = lambda xs: jnp.stack(xs, axis=0)
    o, s_new = _dn_chunks(s_ref[...], st(qs), st(ks), st(vs), st(betas), st(gcols), st(grows))
    s_ref[...] = s_new
    for h in range(DN_HEADS):
        of_ref[:, h * DN_HD : (h + 1) * DN_HD] = o[h].astype(of_ref.dtype)
        ob_ref[:, h * DN_HD : (h + 1) * DN_HD] = o[DN_HEADS + h].astype(ob_ref.dtype)


def _dn_scan(q, k, v, gates, alog_row, dtb_row):
    bsz, seq, _ = q.shape
    n_chunks = seq // DN_CHUNK
    fwd = lambda w, col=0: pl.BlockSpec((None, DN_CHUNK, w), lambda b, i: (b, i, col))
    bwd = lambda w, col=0: pl.BlockSpec((None, DN_CHUNK, w), lambda b, i: (b, n_chunks - 1 - i, col))
    gcol = PROJ_DNS // 128
    row = pl.BlockSpec((1, 128), lambda b, i: (0, 0))
    out = jax.ShapeDtypeStruct((bsz, seq, MIX), bf16)
    return pl.pallas_call(
        _dn_scan_kernel,
        out_shape=[out, out],
        grid=(bsz, n_chunks),
        in_specs=[fwd(MIX)] * 3 + [fwd(128, gcol)] + [bwd(MIX)] * 3 + [bwd(128, gcol)] + [row, row],
        out_specs=[fwd(MIX), bwd(MIX)],
        scratch_shapes=[pltpu.VMEM((2 * DN_HEADS, DN_HD, DN_HD), f32)],
        compiler_params=_cparams(2),
        name="dn_scan",
    )(q, k, v, gates, q, k, v, gates, alog_row, dtb_row)


def _dn_post_kernel(of_ref, ob_ref, z_ref, nw_ref, avg_ref, o_ref):
    o = of_ref[...].astype(f32) + ob_ref[...].astype(f32)
    o = o * lax.rsqrt(_mm2(o * o, avg_ref[...]) + NORM_EPS) * nw_ref[...]
    o_ref[...] = (o * _silu(z_ref[...])).astype(o_ref.dtype)


def _dn_post(of, ob, c_dn, norm_w, avg128, tm=512):
    t = of.shape[0]
    tm = min(tm, t)
    tok = pl.BlockSpec((tm, MIX), lambda i: (i, 0))
    return pl.pallas_call(
        _dn_post_kernel,
        out_shape=jax.ShapeDtypeStruct((t, MIX), bf16),
        grid=(t // tm,),
        in_specs=[tok, tok, pl.BlockSpec((tm, MIX), lambda i: (i, PROJ_DN // MIX + 3)),
                  pl.BlockSpec((1, MIX), lambda i: (0, 0)), pl.BlockSpec((MIX, MIX), lambda i: (0, 0))],
        out_specs=tok,
        compiler_params=_cparams(1),
        name="dn_post",
    )(of, ob, c_dn, norm_w, avg128)


S5_SG = 8
S5_NSG = S5_G // S5_SG
S5_ST = S5_SG * S5_P
S5_W = S5_CHUNK * 128
S5_TOK = 512


def _s5_proj_kernel(x_ref, g_ref, w_ref, o_ref):
    h = _rms(x_ref[...], g_ref[...]).astype(bf16)
    y = jnp.dot(h, w_ref[...], preferred_element_type=f32)
    o_ref[...] = y.reshape(o_ref.shape)


def _s5_proj(x3, g, w):
    bsz, seq, _ = x3.shape
    tok = min(S5_TOK, seq)
    nck = tok // S5_CHUNK
    return pl.pallas_call(
        _s5_proj_kernel,
        out_shape=jax.ShapeDtypeStruct((seq // S5_CHUNK, bsz, S5_CHUNK, MIX), f32),
        grid=(bsz, seq // tok),
        in_specs=[
            pl.BlockSpec((None, tok, D_MODEL), lambda b, i: (b, i, 0)),
            pl.BlockSpec((1, D_MODEL), lambda b, i: (0, 0)),
            pl.BlockSpec((D_MODEL, MIX), lambda b, i: (0, 0)),
        ],
        out_specs=pl.BlockSpec((nck, None, S5_CHUNK, MIX), lambda b, i: (i, b, 0, 0)),
        compiler_params=_cparams(2),
        name="s5_proj",
    )(x3, g, w)


def _mm3r(a, b01):
    hi = a.astype(bf16)
    r1 = a - hi.astype(f32)
    mid = r1.astype(bf16)
    lo = (r1 - mid.astype(f32)).astype(bf16)
    dot = lambda p: jnp.dot(p, b01, preferred_element_type=f32)
    return dot(hi) + dot(mid) + dot(lo)


def _s5_build_operators(d, bb_ref, cc_ref, prow_ref, pcol_ref, t_ref, f_ref, e_ref):
    c = S5_CHUNK
    io = lambda shape, ax: lax.broadcasted_iota(jnp.int32, shape, ax)
    tile_f = (io((S5_P, S5_ST), 0) == (io((S5_P, S5_ST), 1) & (S5_P - 1))).astype(bf16)
    mask_f = (io((128, S5_ST), 0) >> 4) == (io((128, S5_ST), 1) >> 6)
    bbr = jnp.where(mask_f, _mm3r(bb_ref[0], tile_f), 0.0)
    bbi = jnp.where(mask_f, _mm3r(bb_ref[1], tile_f), 0.0)
    tile_e = (io((S5_CG, 128), 0) == (io((S5_CG, 128), 1) & (S5_CG - 1))).astype(bf16)
    mask_e = (io((S5_ST, 128), 0) >> 6) == (io((S5_ST, 128), 1) >> 4)
    cre = jnp.where(mask_e, _mm3r(cc_ref[0], tile_e), 0.0)
    cim = jnp.where(mask_e, _mm3r(cc_ref[1], tile_e), 0.0)
    for j in range(c):
        pr = prow_ref[0, j : j + 1, :]
        pi = prow_ref[1, j : j + 1, :]
        f_ref[j * 128 : (j + 1) * 128, :S5_ST] = (pr * bbr - pi * bbi).astype(bf16)
        f_ref[j * 128 : (j + 1) * 128, S5_ST:] = (pr * bbi + pi * bbr).astype(bf16)
    pcr = pcol_ref[0]
    pci = pcol_ref[1]

    def c_times_power(col):
        pr = jnp.broadcast_to(pcr[:, col : col + 1], (S5_ST, 128))
        pi = jnp.broadcast_to(pci[:, col : col + 1], (S5_ST, 128))
        return cre * pr - cim * pi, -(cre * pi + cim * pr)

    for i in range(c):
        er, ei = c_times_power(i)
        e_ref[:S5_ST, i * 128 : (i + 1) * 128] = er.astype(bf16)
        e_ref[S5_ST:, i * 128 : (i + 1) * 128] = ei.astype(bf16)
    bb = jnp.concatenate([bbr, bbi], axis=1)
    ks = []
    for lag in range(c):
        er, ei = c_times_power(c + lag)
        ks.append(_mm(bb, jnp.concatenate([er, ei], axis=0)).astype(bf16))
    t_ref[...] = jnp.zeros_like(t_ref)

    @pl.when(d == 0)
    def _():
        for lag in range(c):
            for j in range(c - lag):
                t_ref[j * 128 : (j + 1) * 128, (j + lag) * 128 : (j + lag + 1) * 128] = ks[lag]

    @pl.when(d == 1)
    def _():
        for lag in range(c):
            for j in range(c - lag):
                t_ref[(j + lag) * 128 : (j + lag + 1) * 128, j * 128 : (j + 1) * 128] = ks[lag]


def _s5_kernel(u_ref, bb_ref, cc_ref, prow_ref, pcol_ref, y_ref, t_ref, f_ref, e_ref, lhs_ref, vr_ref, vi_ref, xr_ref, xi_ref, *, mh, bsz):
    c = S5_CHUNK
    d = pl.program_id(1)

    @pl.when(pl.program_id(2) == 0)
    def _():
        _s5_build_operators(d, bb_ref, cc_ref, prow_ref, pcol_ref, t_ref, f_ref, e_ref)
        xr_ref[...] = jnp.zeros_like(xr_ref)
        xi_ref[...] = jnp.zeros_like(xi_ref)

    for j in range(c):
        lhs_ref[:, j * 128 : (j + 1) * 128] = u_ref[pl.ds(j, mh, stride=c), :].astype(bf16)
    lhs = lhs_ref[...]
    y = jnp.dot(lhs, t_ref[...], preferred_element_type=f32)
    vr_ref[...] = jnp.dot(lhs, f_ref[:, :S5_ST], preferred_element_type=f32)
    vi_ref[...] = jnp.dot(lhs, f_ref[:, S5_ST:], preferred_element_type=f32)
    ar = prow_ref[0, c : c + 1, :]
    ai = prow_ref[1, c : c + 1, :]
    n_loc = mh // bsz

    def body(n, carry):
        xr, xi = carry
        ci = jnp.where(d == 0, n, n_loc - 1 - n)
        rows = pl.ds(pl.multiple_of(ci * bsz, bsz), bsz)
        vr = vr_ref[rows, :]
        vi = vi_ref[rows, :]
        vr_ref[rows, :] = xr
        vi_ref[rows, :] = xi
        return ar * xr - ai * xi + vr, ar * xi + ai * xr + vi

    xr, xi = lax.fori_loop(0, n_loc, body, (xr_ref[...], xi_ref[...]))
    xr_ref[...] = xr
    xi_ref[...] = xi
    y = y + jnp.dot(vr_ref[...].astype(bf16), e_ref[:S5_ST, :], preferred_element_type=f32)
    y = y + jnp.dot(vi_ref[...].astype(bf16), e_ref[S5_ST:, :], preferred_element_type=f32)
    for i in range(c):
        y_ref[pl.ds(i, mh, stride=c), :] = y[:, i * 128 : (i + 1) * 128]


def _s5_core(u4, bb, cc, prow, pcol):
    n_chunks, bsz, c, _ = u4.shape
    rows = n_chunks * bsz * c
    n_part = 4 if n_chunks % 4 == 0 else 1
    rp = rows // n_part
    mh = rp // c
    part = lambda d, h: h + d * (n_part - 1 - 2 * h)
    per_tile = lambda a: pl.BlockSpec((None, None) + a.shape[2:], lambda s, d, h: (d, s) + (0,) * (a.ndim - 2))
    return pl.pallas_call(
        functools.partial(_s5_kernel, mh=mh, bsz=bsz),
        out_shape=jax.ShapeDtypeStruct((2, rows, MIX), f32),
        grid=(S5_NSG, 2, n_part),
        in_specs=[
            pl.BlockSpec((rp, 128), lambda s, d, h: (part(d, h), s)),
            per_tile(bb), per_tile(cc), per_tile(prow), per_tile(pcol),
        ],
        out_specs=pl.BlockSpec((None, rp, 128), lambda s, d, h: (d, part(d, h), s)),
        scratch_shapes=[
            pltpu.VMEM((S5_W, S5_W), bf16),
            pltpu.VMEM((S5_W, 2 * S5_ST), bf16),
            pltpu.VMEM((2 * S5_ST, S5_W), bf16),
            pltpu.VMEM((mh, S5_W), bf16),
            pltpu.VMEM((mh, S5_ST), f32),
            pltpu.VMEM((mh, S5_ST), f32),
            pltpu.VMEM((bsz, S5_ST), f32),
            pltpu.VMEM((bsz, S5_ST), f32),
        ],
        compiler_params=_cparams(3),
        name="s5_core",
    )(u4.reshape(rows, MIX), bb, cc, prow, pcol)


def _s5_post_kernel(yf_ref, yb_ref, u_ref, d_ref, w_ref, b_ref, o_ref):
    n = o_ref.shape[0]
    y = (yf_ref[...] + yb_ref[...] + d_ref[...] * u_ref[...]).reshape(n, MIX)
    y = 0.5 * y * (1.0 + jnp.tanh(math.sqrt(2.0 / math.pi) * (y + 0.044715 * (y * y * y))))
    o_ref[...] = (y * _sigmoid(_mm(y, w_ref[...]) + b_ref[...])).astype(o_ref.dtype)


def _s5_post(y2, u4, d_skip, glu_w, glu_b):
    n_chunks, bsz, c, _ = u4.shape
    seq = n_chunks * c
    tok = min(S5_TOK, seq)
    nck = tok // c
    y5 = y2.reshape(2, n_chunks, bsz, c, MIX)
    row = pl.BlockSpec((1, MIX), lambda b, i: (0, 0))
    return pl.pallas_call(
        _s5_post_kernel,
        out_shape=jax.ShapeDtypeStruct((bsz * seq, MIX), bf16),
        grid=(bsz, seq // tok),
        in_specs=[
            pl.BlockSpec((None, nck, None, c, MIX), lambda b, i: (0, i, b, 0, 0)),
            pl.BlockSpec((None, nck, None, c, MIX), lambda b, i: (1, i, b, 0, 0)),
            pl.BlockSpec((nck, None, c, MIX), lambda b, i: (i, b, 0, 0)),
            row,
            pl.BlockSpec((MIX, MIX), lambda b, i: (0, 0)),
            row,
        ],
        out_specs=pl.BlockSpec((tok, MIX), lambda b, i: (b * (seq // tok) + i, 0)),
        compiler_params=_cparams(2),
        name="s5_post",
    )(y5, y5, u4, d_skip, glu_w, glu_b)


def _s5_tables(a_re, a_im, log_dt, b_re, b_im, c_re, c_im):
    c = S5_CHUNK
    dt = jnp.exp(log_dt)[..., None]
    mag = a_re * dt
    ph = a_im * dt
    steps = jnp.arange(c + 1, dtype=f32)[:, None, None, None]
    m = jnp.exp(mag[None] * steps)
    pr = m * jnp.cos(ph[None] * steps)
    pi = m * jnp.sin(ph[None] * steps)
    den = a_re * a_re + a_im * a_im
    qr = ((pr[1] - 1.0) * a_re + pi[1] * a_im) / den
    qi = (pi[1] * a_re - (pr[1] - 1.0) * a_im) / den
    bbr = (qr[..., None] * b_re - qi[..., None] * b_im).transpose(0, 1, 3, 2)
    bbi = (qr[..., None] * b_im + qi[..., None] * b_re).transpose(0, 1, 3, 2)
    bb = jnp.stack([bbr, bbi], axis=1).reshape(2, 2, S5_NSG, S5_SG * S5_CG, S5_P).transpose(0, 2, 1, 3, 4)
    cc = jnp.stack([c_re, c_im], axis=0).transpose(0, 1, 3, 2).reshape(2, S5_NSG, S5_ST, S5_CG)
    cc = jnp.broadcast_to(cc.transpose(1, 0, 2, 3)[None], (2, S5_NSG, 2, S5_ST, S5_CG))
    d_idx = jnp.arange(2)[:, None]
    steps_i = jnp.arange(c)
    fsel = jnp.stack([c - 1 - steps_i, steps_i], axis=0)
    fsel = jnp.concatenate([fsel, jnp.full((2, 1), c)], axis=1)
    tab = lambda p, sel: p[sel, d_idx].reshape(2, sel.shape[1], S5_NSG, S5_ST)
    prow = jnp.stack([tab(pr, fsel), tab(pi, fsel)], axis=1).transpose(0, 3, 1, 2, 4)
    prow = jnp.concatenate([prow, jnp.zeros((2, S5_NSG, 2, 24 - c - 1, S5_ST), f32)], axis=3)
    esel = jnp.stack([steps_i + 1, c - steps_i], axis=0)
    esel = jnp.concatenate([esel, jnp.broadcast_to(steps_i, (2, c))], axis=1)
    pcol = jnp.stack([tab(pr, esel), tab(pi, esel)], axis=1).transpose(0, 3, 1, 4, 2)
    return bb, cc, prow, pcol


def _block_diag_ones(block, scale=1.0):
    idx = jnp.arange(MIX) // block
    return ((idx[:, None] == idx[None, :]).astype(f32) * scale).astype(bf16)


def kernel(x, p, ffn1_norm, ffn1_w1, ffn1_w3, ffn1_w2, mix_norm, w_in, rwkv_mu, rwkv_w0, rwkv_w2, rwkv_a0, rwkv_a2, rwkv_g2, rwkv_k_k, rwkv_k_a, rwkv_r_k, rwkv_ln_w, rwkv_ln_b, dn_conv, dn_a_log, dn_dt_bias, dn_norm, s5_a_re, s5_a_im, s5_log_dt, s5_b_re, s5_b_im, s5_c_re, s5_c_im, s5_d, s5_glu_w, s5_glu_b, w_branch_a, w_branch_b, w_branch_c, w_out, ffn2_norm, ffn2_w1, ffn2_w3, ffn2_w2, ple_norm, ple_w_gate, ple_w_proj, final_norm):
    bsz, seq, _ = x.shape
    t = bsz * seq
    depth = p.shape[0]
    row = lambda a: a.reshape(1, -1).astype(f32)
    ones64 = _block_diag_ones(RW_HD)
    avg64 = _block_diag_ones(RW_HD, 1.0 / RW_HD)
    ones128 = _block_diag_ones(DN_HD)
    avg128 = _block_diag_ones(DN_HD, 1.0 / DN_HD)
    o_rw, o_dn, o_s5, o_gate = 0, 3488, 3488 + 4128, 3488 + 4128 + 1024

    xt = x.reshape(t, D_MODEL)
    for i in range(depth):
        xt = _ffn(xt, row(ffn1_norm[i]), ffn1_w1[i].astype(bf16), ffn1_w3[i].astype(bf16), ffn1_w2[i].astype(bf16))

        wi = w_in[i]
        g_mix = row(mix_norm[i])
        w_all = jnp.concatenate([
            wi[:, o_rw : o_rw + 3488], jnp.zeros((D_MODEL, PROJ_DNS - 3488), f32),
            wi[:, o_dn + 4 * MIX : o_dn + 4128], jnp.zeros((D_MODEL, PROJ_DN - PROJ_DNS - 4 * DN_HEADS), f32),
            wi[:, o_dn : o_dn + 4 * MIX]], axis=1).astype(bf16)
        w_s5 = wi[:, o_s5 : o_s5 + MIX].astype(bf16)
        w_gate = wi[:, o_gate:].astype(bf16)
        c_all = _norm_mm(xt, g_mix, w_all).reshape(bsz, seq, PROJ_COLS)

        mu = jnp.concatenate([rwkv_mu[i], jnp.zeros((RW_COLS - 3488,), f32)]).reshape(1, RW_COLS)
        zeros_blk = jnp.zeros((64, MIX), f32)
        w2cat = jnp.concatenate([
            jnp.concatenate([rwkv_w2[i, 0], zeros_blk], axis=1),
            jnp.concatenate([zeros_blk, rwkv_w2[i, 1]], axis=1)], axis=0).astype(bf16)
        a2cat = jnp.concatenate([
            jnp.concatenate([rwkv_a2[i, 0], zeros_blk], axis=1),
            jnp.concatenate([zeros_blk, rwkv_a2[i, 1]], axis=1)], axis=0).astype(bf16)
        g2pad = jnp.concatenate([rwkv_g2[i], jnp.zeros((256 - 160, MIX), f32)], axis=0).astype(bf16)
        prep = _rwkv_prep(c_all, mu, row(rwkv_w0[i]), row(rwkv_a0[i]), w2cat, a2cat, g2pad,
                          row(rwkv_k_k[i]), row(rwkv_k_a[i]), row(rwkv_r_k[i]), ones64)
        r_, v_, an_, lwf_, lwb_, kf_, kb_, bf_, bb_, gate_, bonus_ = prep
        yf_rw, yb_rw = _rwkv_scan(r_, v_, an_, lwf_, lwb_, kf_, kb_, bf_, bb_)
        ya = _rwkv_post(yf_rw.reshape(t, MIX), yb_rw.reshape(t, MIX), bonus_.reshape(t, MIX), gate_.reshape(t, MIX),
                        row(rwkv_ln_w[i]), row(rwkv_ln_b[i]), avg64)

        conv8 = jnp.concatenate([dn_conv[i], jnp.zeros((8 - DN_CONV, 3 * MIX), f32)], axis=0)
        q_, k_, v2_ = _dn_prep(c_all, conv8, ones128)
        pad16 = lambda a: jnp.concatenate([jnp.zeros((2 * DN_HEADS,), f32), a.reshape(-1), jnp.zeros((128 - 4 * DN_HEADS,), f32)]).reshape(1, 128)
        of_dn, ob_dn = _dn_scan(q_, k_, v2_, c_all, pad16(dn_a_log[i]), pad16(dn_dt_bias[i]))
        yb = _dn_post(of_dn.reshape(t, MIX), ob_dn.reshape(t, MIX), c_all.reshape(t, PROJ_COLS), jnp.tile(dn_norm[i], DN_HEADS).reshape(1, MIX), avg128)

        u4 = _s5_proj(xt.reshape(bsz, seq, D_MODEL), g_mix, w_s5)
        ops5 = _s5_tables(s5_a_re[i], s5_a_im[i], s5_log_dt[i], s5_b_re[i], s5_b_im[i], s5_c_re[i], s5_c_im[i])
        yc = _s5_post(_s5_core(u4, *ops5), u4, row(s5_d[i]), s5_glu_w[i].astype(bf16), row(s5_glu_b[i]))

        xt = _merge(xt, g_mix, ya, yb, yc, w_gate, w_branch_a[i].astype(bf16), w_branch_b[i].astype(bf16),
                    w_branch_c[i].astype(bf16), w_out[i].astype(bf16))
        xt = _ffn(xt, row(ffn2_norm[i]), ffn2_w1[i].astype(bf16), ffn2_w3[i].astype(bf16), ffn2_w2[i].astype(bf16))
        xt = _ple(xt, p[i].reshape(t, PLE_DIM), row(ple_norm[i]), ple_w_gate[i].astype(bf16), ple_w_proj[i].astype(bf16))
    return _final_norm(xt, row(final_norm)).reshape(bsz, seq, D_MODEL)
```
